```python
import jax, jax.numpy as jnp
from jax import lax
import numpy as np

D_MODEL = 2048
BATCH = 4
SEQ = 8192
DEPTH = 1

D_FF = 5632
CONV_WIDTH = D_MODEL
CONV_K = 3
N_Q_HEADS = 32
N_KV_HEADS = 4
HEAD_DIM = 64
Q_GROUP = N_Q_HEADS // N_KV_HEADS
ATTN_WIDTH = N_Q_HEADS * HEAD_DIM
KV_WIDTH = N_KV_HEADS * HEAD_DIM
WINDOW = 128
BLOCK = 128
RMS_EPS = 1e-5
ATTN_SCALE = HEAD_DIM ** -0.5
IN_COLS = (CONV_WIDTH, CONV_WIDTH, CONV_WIDTH, ATTN_WIDTH, KV_WIDTH, KV_WIDTH, D_MODEL, D_MODEL)
D_IN_PROJ = sum(IN_COLS)

kernel_name = "hybrid_gated_shortconv_swa_macaron"


def rms_norm(x, g):
    x32 = x.astype(jnp.float32)
    y = x32 * lax.rsqrt(jnp.mean(x32 * x32, axis=-1, keepdims=True) + RMS_EPS)
    return (y * g.astype(jnp.float32)).astype(x.dtype)


def swiglu(x, w_gate, w_up, w_down):
    return (jax.nn.silu(x @ w_gate) * (x @ w_up)) @ w_down


def short_gated_conv(b, c, xc, w_conv):
    u = c * xc
    seq = u.shape[1]
    u_pad = jnp.pad(u, ((0, 0), (CONV_K - 1, 0), (0, 0)))
    conv = w_conv[CONV_K - 1] * u
    for tap in range(CONV_K - 1):
        conv = conv + w_conv[tap] * u_pad[:, tap:tap + seq]
    return b * conv


def alibi_slopes(n_heads):
    return 2.0 ** (-8.0 * jnp.arange(1, n_heads + 1, dtype=jnp.float32) / n_heads)


def _swa_single(q, k, v, sinks):
    seq = q.shape[0]
    n_blk = seq // BLOCK
    q = q.reshape(n_blk, BLOCK, N_KV_HEADS, Q_GROUP, HEAD_DIM)
    k = k.reshape(n_blk, BLOCK, N_KV_HEADS, HEAD_DIM)
    v = v.reshape(n_blk, BLOCK, N_KV_HEADS, HEAD_DIM)
    prev = lambda t: jnp.concatenate([jnp.zeros_like(t[:1]), t[:-1]], axis=0)
    k_win = jnp.concatenate([prev(k), k], axis=1)
    v_win = jnp.concatenate([prev(v), v], axis=1)
    scores = jnp.einsum('nqhgd,nshd->nhgqs', q, k_win).astype(jnp.float32) * ATTN_SCALE
    qi = jnp.arange(BLOCK)[:, None]
    kj = jnp.arange(2 * BLOCK)[None, :]
    dist = qi - kj + BLOCK
    band = (dist >= 0) & (dist < WINDOW)
    key_pos = jnp.arange(n_blk)[:, None] * BLOCK - BLOCK + kj
    valid = band[None] & (key_pos >= 0)[:, None, :]
    slopes = alibi_slopes(N_Q_HEADS).reshape(N_KV_HEADS, Q_GROUP)
    bias = -slopes[:, :, None, None] * dist.astype(jnp.float32)
    scores = jnp.where(valid[:, None, None], scores + bias[None], -jnp.inf)
    sink = sinks.astype(jnp.float32).reshape(N_KV_HEADS, Q_GROUP)[None, :, :, None, None]
    m = jnp.maximum(jnp.max(scores, axis=-1, keepdims=True), sink)
    p = jnp.exp(scores - m)
    denom = jnp.sum(p, axis=-1, keepdims=True) + jnp.exp(sink - m)
    p = (p / denom).astype(v.dtype)
    out = jnp.einsum('nhgqs,nshd->nqhgd', p, v_win)
    return out.reshape(seq, ATTN_WIDTH)


def sliding_window_attention(q, k, v, sinks):
    return lax.map(lambda qkv: _swa_single(qkv[0], qkv[1], qkv[2], sinks), (q, k, v))


def gated_parallel_mixer(u, w_in, w_conv, w_conv_out, attn_sinks, w_attn_out, w_out):
    proj = u @ w_in
    split_idx = [int(i) for i in np.cumsum(IN_COLS)[:-1]]
    b, c, xc, q, k, v, g_conv, g_attn = jnp.split(proj, split_idx, axis=-1)
    y_conv = short_gated_conv(b, c, xc, w_conv) @ w_conv_out
    y_attn = sliding_window_attention(q, k, v, attn_sinks) @ w_attn_out
    merged = jax.nn.sigmoid(g_conv) * y_conv + jax.nn.sigmoid(g_attn) * y_attn
    return merged @ w_out


def setup_inputs(seed: int = 0) -> dict:
    key = jax.random.key(seed)
    ks = jax.random.split(key, 20)
    f32 = jnp.float32
    nrm = lambda k, shape, fan_in: jax.random.normal(k, shape, f32) * (fan_in ** -0.5)
    gain = lambda k, shape: 1.0 + 0.02 * jax.random.normal(k, shape, f32)
    L = DEPTH
    return {
        "x": jax.random.normal(ks[0], (BATCH, SEQ, D_MODEL), f32),
        "norm_ffn1": gain(ks[1], (L, D_MODEL)),
        "w_gate1": nrm(ks[2], (L, D_MODEL, D_FF), D_MODEL),
        "w_up1": nrm(ks[3], (L, D_MODEL, D_FF), D_MODEL),
        "w_down1": nrm(ks[4], (L, D_FF, D_MODEL), D_FF),
        "norm_mix": gain(ks[5], (L, D_MODEL)),
        "w_in": nrm(ks[6], (L, D_MODEL, D_IN_PROJ), D_MODEL),
        "w_conv": nrm(ks[7], (L, CONV_K, CONV_WIDTH), CONV_K),
        "w_conv_out": nrm(ks[8], (L, CONV_WIDTH, D_MODEL), CONV_WIDTH),
        "attn_sinks": jax.random.normal(ks[9], (L, N_Q_HEADS), f32),
        "w_attn_out": nrm(ks[10], (L, ATTN_WIDTH, D_MODEL), ATTN_WIDTH),
        "w_out": nrm(ks[11], (L, D_MODEL, D_MODEL), D_MODEL),
        "norm_ffn2": gain(ks[12], (L, D_MODEL)),
        "w_gate2": nrm(ks[13], (L, D_MODEL, D_FF), D_MODEL),
        "w_up2": nrm(ks[14], (L, D_MODEL, D_FF), D_MODEL),
        "w_down2": nrm(ks[15], (L, D_FF, D_MODEL), D_FF),
        "norm_final": gain(ks[16], (D_MODEL,)),
    }


def reference(x, norm_ffn1, w_gate1, w_up1, w_down1, norm_mix, w_in, w_conv, w_conv_out,
              attn_sinks, w_attn_out, w_out, norm_ffn2, w_gate2, w_up2, w_down2, norm_final):
    h = x
    for l in range(DEPTH):
        h = h + 0.5 * swiglu(rms_norm(h, norm_ffn1[l]), w_gate1[l], w_up1[l], w_down1[l])
        h = h + gated_parallel_mixer(rms_norm(h, norm_mix[l]), w_in[l], w_conv[l], w_conv_out[l],
                                     attn_sinks[l], w_attn_out[l], w_out[l])
        h = h + 0.5 * swiglu(rms_norm(h, norm_ffn2[l]), w_gate2[l], w_up2[l], w_down2[l])
    return rms_norm(h, norm_final)
```

```python
import functools

import jax
import jax.numpy as jnp
from jax import lax
from jax.experimental import pallas as pl
from jax.experimental.pallas import tpu as pltpu

D_MODEL = 2048
D_FF = 5632
CONV_K = 3
N_Q_HEADS = 32
N_KV_HEADS = 4
HEAD_DIM = 64
Q_GROUP = N_Q_HEADS // N_KV_HEADS
GROUP_WIDTH = Q_GROUP * HEAD_DIM
WINDOW = 128
RMS_EPS = 1e-5
ATTN_SCALE = HEAD_DIM ** -0.5

LANES = 128
SUBLANES = 8
VMEM_LIMIT_BYTES = 56 * 1024 * 1024

ROW_TILE = 512
FF_TILE = 512
CONV_TILE = 512

_f32 = jnp.float32
_bf16 = jnp.bfloat16


def _rms_norm_f32(x, gain):
    return x * lax.rsqrt(jnp.mean(x * x, axis=-1, keepdims=True) + RMS_EPS) * gain


def _dot(a, b):
    return jnp.dot(a, b, preferred_element_type=_f32)


def _ffn_kernel(h_ref, gain_ref, wg_ref, wu_ref, wd_ref, gfin_ref, out_ref, u_ref, acc_ref,
                *, final_norm):
    j = pl.program_id(1)

    @pl.when(j == 0)
    def _():
        u_ref[...] = _rms_norm_f32(h_ref[...], gain_ref[...]).astype(_bf16)
        acc_ref[...] = jnp.zeros_like(acc_ref)

    u = u_ref[...]
    g = _dot(u, wg_ref[...])
    up = _dot(u, wu_ref[...])
    a = (g * jax.nn.sigmoid(g) * up).astype(_bf16)
    acc_ref[...] += _dot(a, wd_ref[...])

    @pl.when(j == pl.num_programs(1) - 1)
    def _():
        h = h_ref[...] + 0.5 * acc_ref[...]
        if final_norm:
            h = _rms_norm_f32(h, gfin_ref[...])
        out_ref[...] = h


def _ffn(h, gain, wg, wu, wd, gfin, final_norm):
    n = h.shape[0]
    row = lambda i, j: (i, 0)
    const = lambda i, j: (0, 0)
    return pl.pallas_call(
        functools.partial(_ffn_kernel, final_norm=final_norm),
        grid=(n // ROW_TILE, D_FF // FF_TILE),
        in_specs=[
            pl.BlockSpec((ROW_TILE, D_MODEL), row),
            pl.BlockSpec((1, D_MODEL), const),
            pl.BlockSpec((D_MODEL, FF_TILE), lambda i, j: (0, j)),
            pl.BlockSpec((D_MODEL, FF_TILE), lambda i, j: (0, j)),
            pl.BlockSpec((FF_TILE, D_MODEL), lambda i, j: (j, 0)),
            pl.BlockSpec((1, D_MODEL), const),
        ],
        out_specs=pl.BlockSpec((ROW_TILE, D_MODEL), row),
        out_shape=jax.ShapeDtypeStruct((n, D_MODEL), _f32),
        scratch_shapes=[
            pltpu.VMEM((ROW_TILE, D_MODEL), _bf16),
            pltpu.VMEM((ROW_TILE, D_MODEL), _f32),
        ],
        compiler_params=pltpu.CompilerParams(
            dimension_semantics=("arbitrary", "arbitrary"),
            vmem_limit_bytes=VMEM_LIMIT_BYTES),
        name="ffn_final" if final_norm else "ffn",
    )(h, gain, wg, wu, wd, gfin)


def _conv_kernel(h_ref, gain_ref, wb_ref, wc_ref, wx_ref, wgc_ref, taps_ref, wco_ref, out_ref,
                 u_ref, acc_ref, gate_ref, ucbuf_ref, carry_ref, *, tiles_per_seq):
    i = pl.program_id(0)
    j = pl.program_id(1)

    @pl.when(j == 0)
    def _():
        u_ref[...] = _rms_norm_f32(h_ref[...], gain_ref[...]).astype(_bf16)
        acc_ref[...] = jnp.zeros_like(acc_ref)

    @pl.when(i % tiles_per_seq == 0)
    def _():
        carry_ref[j] = jnp.zeros((SUBLANES, CONV_TILE), _f32)

    u = u_ref[...]
    b = _dot(u, wb_ref[...])
    uc = _dot(u, wc_ref[...]) * _dot(u, wx_ref[...])
    ucbuf_ref[0:SUBLANES, :] = carry_ref[j]
    ucbuf_ref[SUBLANES:, :] = uc
    carry_ref[j] = uc[ROW_TILE - SUBLANES:, :]
    taps = taps_ref[...]
    conv = taps[CONV_K - 1:CONV_K, :] * uc
    for tap in range(CONV_K - 1):
        back = CONV_K - 1 - tap
        conv = conv + taps[tap:tap + 1, :] * ucbuf_ref[SUBLANES - back:SUBLANES - back + ROW_TILE, :]
    acc_ref[...] += _dot((b * conv).astype(_bf16), wco_ref[...])

    col = pl.multiple_of(j * CONV_TILE, CONV_TILE)
    gate_ref[:, pl.ds(col, CONV_TILE)] = jax.nn.sigmoid(_dot(u, wgc_ref[...]))

    @pl.when(j == pl.num_programs(1) - 1)
    def _():
        out_ref[...] = (gate_ref[...] * acc_ref[...]).astype(_bf16)


def _conv_branch(h, gain, w_in, taps, wco, tiles_per_seq):
    n = h.shape[0]
    nblk = D_MODEL // CONV_TILE
    row = lambda i, j: (i, 0)
    const = lambda i, j: (0, 0)
    gc_off = (4 * D_MODEL + 2 * N_KV_HEADS * HEAD_DIM) // CONV_TILE
    wcol = lambda blk_off: (lambda i, j: (0, j + blk_off))
    return pl.pallas_call(
        functools.partial(_conv_kernel, tiles_per_seq=tiles_per_seq),
        grid=(n // ROW_TILE, nblk),
        in_specs=[
            pl.BlockSpec((ROW_TILE, D_MODEL), row),
            pl.BlockSpec((1, D_MODEL), const),
            pl.BlockSpec((D_MODEL, CONV_TILE), wcol(0)),
            pl.BlockSpec((D_MODEL, CONV_TILE), wcol(nblk)),
            pl.BlockSpec((D_MODEL, CONV_TILE), wcol(2 * nblk)),
            pl.BlockSpec((D_MODEL, CONV_TILE), wcol(gc_off)),
            pl.BlockSpec((CONV_K, CONV_TILE), lambda i, j: (0, j)),
            pl.BlockSpec((CONV_TILE, D_MODEL), lambda i, j: (j, 0)),
        ],
        out_specs=pl.BlockSpec((ROW_TILE, D_MODEL), row),
        out_shape=jax.ShapeDtypeStruct((n, D_MODEL), _bf16),
        scratch_shapes=[
            pltpu.VMEM((ROW_TILE, D_MODEL), _bf16),
            pltpu.VMEM((ROW_TILE, D_MODEL), _f32),
            pltpu.VMEM((ROW_TILE, D_MODEL), _f32),
            pltpu.VMEM((ROW_TILE + SUBLANES, CONV_TILE), _f32),
            pltpu.VMEM((nblk, SUBLANES, CONV_TILE), _f32),
        ],
        compiler_params=pltpu.CompilerParams(
            dimension_semantics=("arbitrary", "arbitrary"),
            vmem_limit_bytes=VMEM_LIMIT_BYTES),
        name="conv_branch",
    )(h, gain, w_in, w_in, w_in, w_in, taps, wco)


def _attn_kernel(slopes_ref, sinks_ref, h_ref, gain_ref, mconv_ref, wq_ref, wkv_ref, wga_ref, wao_ref,
                 out_ref, u_ref, acc_ref, gate_ref, q_ref, a_ref, klo_ref, khi_ref, vlo_ref, vhi_ref,
                 *, tiles_per_seq):
    i = pl.program_id(0)
    j = pl.program_id(1)
    blocks = ROW_TILE // WINDOW

    @pl.when(j == 0)
    def _():
        u_ref[...] = _rms_norm_f32(h_ref[...], gain_ref[...]).astype(_bf16)
        acc_ref[...] = jnp.zeros_like(acc_ref)

    u = u_ref[...]
    q_ref[...] = (_dot(u, wq_ref[...]) * ATTN_SCALE).astype(_bf16)

    first_tile = i % tiles_per_seq == 0

    @pl.when(first_tile)
    def _():
        for ref in (klo_ref, khi_ref, vlo_ref, vhi_ref):
            ref[j, 0:WINDOW, :] = jnp.zeros((WINDOW, LANES), _bf16)

    kv = _dot(u, wkv_ref[0])
    vk = pltpu.roll(kv, HEAD_DIM, axis=1)
    low = lax.broadcasted_iota(jnp.int32, kv.shape, 1) < HEAD_DIM
    klo_ref[j, WINDOW:, :] = jnp.where(low, kv, 0.0).astype(_bf16)
    khi_ref[j, WINDOW:, :] = jnp.where(low, 0.0, vk).astype(_bf16)
    vlo_ref[j, WINDOW:, :] = jnp.where(low, vk, 0.0).astype(_bf16)
    vhi_ref[j, WINDOW:, :] = jnp.where(low, 0.0, kv).astype(_bf16)

    qi = lax.broadcasted_iota(jnp.int32, (WINDOW, WINDOW), 0)
    kc = lax.broadcasted_iota(jnp.int32, (WINDOW, WINDOW), 1)
    in_cur = kc <= qi
    dist = (qi - kc + jnp.where(in_cur, 0, WINDOW)).astype(_f32)
    neg_inf = jnp.float32(-jnp.inf)

    for blk in range(blocks):
        r0 = blk * WINDOW
        for pair in range(Q_GROUP // 2):
            q2 = q_ref[r0:r0 + WINDOW, pair * LANES:(pair + 1) * LANES]
            out = None
            for half, (k_ref, v_ref) in enumerate(((klo_ref, vlo_ref), (khi_ref, vhi_ref))):
                head = j * Q_GROUP + 2 * pair + half
                keys = k_ref[j, r0:r0 + 2 * WINDOW, :]
                s = lax.dot_general(q2, keys, (((1,), (1,)), ((), ())), preferred_element_type=_f32)
                s_prev = s[:, :WINDOW]
                if blk == 0:
                    s_prev = jnp.where(first_tile, neg_inf, s_prev)
                s = jnp.where(in_cur, s[:, WINDOW:], s_prev) - slopes_ref[head] * dist
                sink = sinks_ref[head]
                m = jnp.maximum(jnp.max(s, axis=-1, keepdims=True), sink)
                p = jnp.exp(s - m)
                denom = jnp.sum(p, axis=-1, keepdims=True) + jnp.exp(sink - m)
                p2 = jnp.concatenate([jnp.where(in_cur, 0.0, p), jnp.where(in_cur, p, 0.0)], axis=1)
                o = _dot(p2.astype(_bf16), v_ref[j, r0:r0 + 2 * WINDOW, :]) / denom
                out = o if out is None else out + o
            a_ref[r0:r0 + WINDOW, pair * LANES:(pair + 1) * LANES] = out.astype(_bf16)

    for ref in (klo_ref, khi_ref, vlo_ref, vhi_ref):
        ref[j, 0:WINDOW, :] = ref[j, ROW_TILE:ROW_TILE + WINDOW, :]

    acc_ref[...] += _dot(a_ref[...], wao_ref[...])
    col = pl.multiple_of(j * GROUP_WIDTH, GROUP_WIDTH)
    gate_ref[:, pl.ds(col, GROUP_WIDTH)] = jax.nn.sigmoid(_dot(u, wga_ref[...]))

    @pl.when(j == pl.num_programs(1) - 1)
    def _():
        merged = mconv_ref[...].astype(_f32) + gate_ref[...] * acc_ref[...]
        out_ref[...] = merged.astype(_bf16)


def _attn_branch(h, gain, mconv, w_in, wkv, wao, slopes, sinks, tiles_per_seq):
    n = h.shape[0]
    q_off = 3 * D_MODEL // GROUP_WIDTH
    ga_off = (D_MODEL * 5 + 2 * N_KV_HEADS * HEAD_DIM) // GROUP_WIDTH
    row = lambda i, j: (i, 0)
    const = lambda i, j: (0, 0)
    smem = pl.BlockSpec(memory_space=pltpu.SMEM)
    kv_scratch = pltpu.VMEM((N_KV_HEADS, ROW_TILE + WINDOW, LANES), _bf16)
    return pl.pallas_call(
        functools.partial(_attn_kernel, tiles_per_seq=tiles_per_seq),
        grid=(n // ROW_TILE, N_KV_HEADS),
        in_specs=[
            smem, smem,
            pl.BlockSpec((ROW_TILE, D_MODEL), row),
            pl.BlockSpec((1, D_MODEL), const),
            pl.BlockSpec((ROW_TILE, D_MODEL), row),
            pl.BlockSpec((D_MODEL, GROUP_WIDTH), lambda i, j: (0, j + q_off)),
            pl.BlockSpec((1, D_MODEL, LANES), lambda i, j: (j, 0, 0)),
            pl.BlockSpec((D_MODEL, GROUP_WIDTH), lambda i, j: (0, j + ga_off)),
            pl.BlockSpec((GROUP_WIDTH, D_MODEL), lambda i, j: (j, 0)),
        ],
        out_specs=pl.BlockSpec((ROW_TILE, D_MODEL), row),
        out_shape=jax.ShapeDtypeStruct((n, D_MODEL), _bf16),
        scratch_shapes=[
            pltpu.VMEM((ROW_TILE, D_MODEL), _bf16),
            pltpu.VMEM((ROW_TILE, D_MODEL), _f32),
            pltpu.VMEM((ROW_TILE, D_MODEL), _f32),
            pltpu.VMEM((ROW_TILE, GROUP_WIDTH), _bf16),
            pltpu.VMEM((ROW_TILE, GROUP_WIDTH), _bf16),
            kv_scratch, kv_scratch, kv_scratch, kv_scratch,
        ],
        compiler_params=pltpu.CompilerParams(
            dimension_semantics=("arbitrary", "arbitrary"),
            vmem_limit_bytes=VMEM_LIMIT_BYTES),
        name="attn_branch",
    )(slopes, sinks, h, gain, mconv, w_in, wkv, w_in, wao)


def _oproj_kernel(h_ref, m_ref, w_ref, out_ref):
    out_ref[...] = h_ref[...] + _dot(m_ref[...], w_ref[...])


def _oproj(h, merged, w_out):
    n = h.shape[0]
    row = lambda i: (i, 0)
    return pl.pallas_call(
        _oproj_kernel,
        grid=(n // ROW_TILE,),
        in_specs=[
            pl.BlockSpec((ROW_TILE, D_MODEL), row),
            pl.BlockSpec((ROW_TILE, D_MODEL), row),
            pl.BlockSpec((D_MODEL, D_MODEL), lambda i: (0, 0)),
        ],
        out_specs=pl.BlockSpec((ROW_TILE, D_MODEL), row),
        out_shape=jax.ShapeDtypeStruct((n, D_MODEL), _f32),
        compiler_params=pltpu.CompilerParams(
            dimension_semantics=("arbitrary",),
            vmem_limit_bytes=VMEM_LIMIT_BYTES),
        name="oproj",
    )(h, merged, w_out)


def kernel(x, norm_ffn1, w_gate1, w_up1, w_down1, norm_mix, w_in, w_conv, w_conv_out, attn_sinks,
           w_attn_out, w_out, norm_ffn2, w_gate2, w_up2, w_down2, norm_final):
    batch, seq, d = x.shape
    depth = norm_ffn1.shape[0]
    assert d == D_MODEL and seq % ROW_TILE == 0 and ROW_TILE % WINDOW == 0
    tiles_per_seq = seq // ROW_TILE
    bf = lambda w: w.astype(_bf16)
    gain = lambda g: g.reshape(1, D_MODEL).astype(_f32)
    slopes = 2.0 ** (-8.0 * jnp.arange(1, N_Q_HEADS + 1, dtype=_f32) / N_Q_HEADS)
    k_off = 4 * D_MODEL
    v_off = k_off + N_KV_HEADS * HEAD_DIM

    h = x.reshape(batch * seq, D_MODEL)
    gfin = gain(norm_final)
    for l in range(depth):
        last = l == depth - 1
        w_in_l = bf(w_in[l])
        wk = w_in_l[:, k_off:v_off].reshape(D_MODEL, N_KV_HEADS, HEAD_DIM)
        wv = w_in_l[:, v_off:v_off + N_KV_HEADS * HEAD_DIM].reshape(D_MODEL, N_KV_HEADS, HEAD_DIM)
        wkv = jnp.concatenate([wk, wv], axis=-1).transpose(1, 0, 2)

        h = _ffn(h, gain(norm_ffn1[l]), bf(w_gate1[l]), bf(w_up1[l]), bf(w_down1[l]), gfin, False)
        g_mix = gain(norm_mix[l])
        mconv = _conv_branch(h, g_mix, w_in_l, w_conv[l].astype(_f32), bf(w_conv_out[l]), tiles_per_seq)
        merged = _attn_branch(h, g_mix, mconv, w_in_l, wkv, bf(w_attn_out[l]), slopes,
                              attn_sinks[l].astype(_f32), tiles_per_seq)
        h = _oproj(h, merged, bf(w_out[l]))
        h = _ffn(h, gain(norm_ffn2[l]), bf(w_gate2[l]), bf(w_up2[l]), bf(w_down2[l]), gfin, last)
    return h.reshape(batch, seq, D_MODEL)
```

```python
import functools

import jax
import jax.numpy as jnp
from jax import lax
from jax.experimental import pallas as pl
from jax.experimental.pallas import tpu as pltpu

D_MODEL = 2048
D_FF = 5632
CONV_K = 3
N_Q_HEADS = 32
N_KV_HEADS = 4
HEAD_DIM = 64
Q_GROUP = N_Q_HEADS // N_KV_HEADS
GROUP_WIDTH = Q_GROUP * HEAD_DIM
WINDOW = 128
RMS_EPS = 1e-5
ATTN_SCALE = HEAD_DIM ** -0.5

LANES = 128
SUBLANES = 8
VMEM_LIMIT_BYTES = 56 * 1024 * 1024

ROW_TILE = 512
FF_TILE = 512
CONV_TILE = 512

_f32 = jnp.float32
_bf16 = jnp.bfloat16


def _rms_norm_f32(x, gain):
    return x * lax.rsqrt(jnp.mean(x * x, axis=-1, keepdims=True) + RMS_EPS) * gain


def _dot(a, b):
    return jnp.dot(a, b, preferred_element_type=_f32)


def _ffn_kernel(*refs, final_norm, mixer_out):
    if mixer_out:
        merged_ref, wout_ref, *refs = refs
    h_ref, gain_ref, wg_ref, wu_ref, wd_ref, gfin_ref, out_ref, u_ref, acc_ref = refs
    j = pl.program_id(1)

    @pl.when(j == 0)
    def _():
        h = h_ref[...]
        if mixer_out:
            h = h + _dot(merged_ref[...], wout_ref[...])
            out_ref[...] = h
        u_ref[...] = _rms_norm_f32(h, gain_ref[...]).astype(_bf16)
        acc_ref[...] = jnp.zeros_like(acc_ref)

    u = u_ref[...]
    g = _dot(u, wg_ref[...])
    up = _dot(u, wu_ref[...])
    a = (g * jax.nn.sigmoid(g) * up).astype(_bf16)
    acc_ref[...] += _dot(a, wd_ref[...])

    @pl.when(j == pl.num_programs(1) - 1)
    def _():
        h = (out_ref[...] if mixer_out else h_ref[...]) + 0.5 * acc_ref[...]
        if final_norm:
            h = _rms_norm_f32(h, gfin_ref[...])
        out_ref[...] = h


def _ffn(h, gain, wg, wu, wd, gfin, final_norm, merged=None, w_out=None):
    n = h.shape[0]
    mixer_out = merged is not None
    row = lambda i, j: (i, 0)
    const = lambda i, j: (0, 0)
    in_specs = [
        pl.BlockSpec((ROW_TILE, D_MODEL), row),
        pl.BlockSpec((1, D_MODEL), const),
        pl.BlockSpec((D_MODEL, FF_TILE), lambda i, j: (0, j)),
        pl.BlockSpec((D_MODEL, FF_TILE), lambda i, j: (0, j)),
        pl.BlockSpec((FF_TILE, D_MODEL), lambda i, j: (j, 0)),
        pl.BlockSpec((1, D_MODEL), const),
    ]
    args = (h, gain, wg, wu, wd, gfin)
    if mixer_out:
        in_specs = [
            pl.BlockSpec((ROW_TILE, D_MODEL), row),
            pl.BlockSpec((D_MODEL, D_MODEL), const, pipeline_mode=pl.Buffered(1)),
        ] + in_specs
        args = (merged, w_out) + args
    return pl.pallas_call(
        functools.partial(_ffn_kernel, final_norm=final_norm, mixer_out=mixer_out),
        grid=(n // ROW_TILE, D_FF // FF_TILE),
        in_specs=in_specs,
        out_specs=pl.BlockSpec((ROW_TILE, D_MODEL), row),
        out_shape=jax.ShapeDtypeStruct((n, D_MODEL), _f32),
        scratch_shapes=[
            pltpu.VMEM((ROW_TILE, D_MODEL), _bf16),
            pltpu.VMEM((ROW_TILE, D_MODEL), _f32),
        ],
        compiler_params=pltpu.CompilerParams(
            dimension_semantics=("arbitrary", "arbitrary"),
            vmem_limit_bytes=VMEM_LIMIT_BYTES),
        name="ffn_mixer_out" if mixer_out else "ffn",
    )(*args)


def _conv_kernel(h_ref, gain_ref, wb_ref, wc_ref, wx_ref, wgc_ref, taps_ref, wco_ref, out_ref,
                 u_ref, acc_ref, gate_ref, ucbuf_ref, carry_ref, *, tiles_per_seq):
    i = pl.program_id(0)
    j = pl.program_id(1)

    @pl.when(j == 0)
    def _():
        u_ref[...] = _rms_norm_f32(h_ref[...], gain_ref[...]).astype(_bf16)
        acc_ref[...] = jnp.zeros_like(acc_ref)

    @pl.when(i % tiles_per_seq == 0)
    def _():
        carry_ref[j] = jnp.zeros((SUBLANES, CONV_TILE), _f32)

    u = u_ref[...]
    b = _dot(u, wb_ref[...])
    uc = _dot(u, wc_ref[...]) * _dot(u, wx_ref[...])
    ucbuf_ref[0:SUBLANES, :] = carry_ref[j]
    ucbuf_ref[SUBLANES:, :] = uc
    carry_ref[j] = uc[ROW_TILE - SUBLANES:, :]
    taps = taps_ref[...]
    conv = taps[CONV_K - 1:CONV_K, :] * uc
    for tap in range(CONV_K - 1):
        back = CONV_K - 1 - tap
        conv = conv + taps[tap:tap + 1, :] * ucbuf_ref[SUBLANES - back:SUBLANES - back + ROW_TILE, :]
    acc_ref[...] += _dot((b * conv).astype(_bf16), wco_ref[...])

    col = pl.multiple_of(j * CONV_TILE, CONV_TILE)
    gate_ref[:, pl.ds(col, CONV_TILE)] = jax.nn.sigmoid(_dot(u, wgc_ref[...]))

    @pl.when(j == pl.num_programs(1) - 1)
    def _():
        out_ref[...] = (gate_ref[...] * acc_ref[...]).astype(_bf16)


def _conv_branch(h, gain, w_in, taps, wco, tiles_per_seq):
    n = h.shape[0]
    nblk = D_MODEL // CONV_TILE
    row = lambda i, j: (i, 0)
    const = lambda i, j: (0, 0)
    gc_off = (4 * D_MODEL + 2 * N_KV_HEADS * HEAD_DIM) // CONV_TILE
    wcol = lambda blk_off: (lambda i, j: (0, j + blk_off))
    return pl.pallas_call(
        functools.partial(_conv_kernel, tiles_per_seq=tiles_per_seq),
        grid=(n // ROW_TILE, nblk),
        in_specs=[
            pl.BlockSpec((ROW_TILE, D_MODEL), row),
            pl.BlockSpec((1, D_MODEL), const),
            pl.BlockSpec((D_MODEL, CONV_TILE), wcol(0)),
            pl.BlockSpec((D_MODEL, CONV_TILE), wcol(nblk)),
            pl.BlockSpec((D_MODEL, CONV_TILE), wcol(2 * nblk)),
            pl.BlockSpec((D_MODEL, CONV_TILE), wcol(gc_off)),
            pl.BlockSpec((CONV_K, CONV_TILE), lambda i, j: (0, j)),
            pl.BlockSpec((CONV_TILE, D_MODEL), lambda i, j: (j, 0)),
        ],
        out_specs=pl.BlockSpec((ROW_TILE, D_MODEL), row),
        out_shape=jax.ShapeDtypeStruct((n, D_MODEL), _bf16),
        scratch_shapes=[
            pltpu.VMEM((ROW_TILE, D_MODEL), _bf16),
            pltpu.VMEM((ROW_TILE, D_MODEL), _f32),
            pltpu.VMEM((ROW_TILE, D_MODEL), _f32),
            pltpu.VMEM((ROW_TILE + SUBLANES, CONV_TILE), _f32),
            pltpu.VMEM((nblk, SUBLANES, CONV_TILE), _f32),
        ],
        compiler_params=pltpu.CompilerParams(
            dimension_semantics=("arbitrary", "arbitrary"),
            vmem_limit_bytes=VMEM_LIMIT_BYTES),
        name="conv_branch",
    )(h, gain, w_in, w_in, w_in, w_in, taps, wco)


def _attn_kernel(slopes_ref, sinks_ref, h_ref, gain_ref, mconv_ref, wq_ref, wkv_ref, wga_ref, wao_ref,
                 out_ref, u_ref, acc_ref, gate_ref, q_ref, a_ref, klo_ref, khi_ref, vlo_ref, vhi_ref,
                 *, tiles_per_seq):
    i = pl.program_id(0)
    j = pl.program_id(1)
    blocks = ROW_TILE // WINDOW
    pairs = Q_GROUP // 2
    first_tile = i % tiles_per_seq == 0
    kv_refs = (klo_ref, khi_ref, vlo_ref, vhi_ref)

    @pl.when(jnp.logical_and(j == 0, first_tile))
    def _():
        for ref in kv_refs:
            ref[:, 0:WINDOW, :] = jnp.zeros((N_KV_HEADS, WINDOW, LANES), _bf16)

    @pl.when(jnp.logical_and(j == 0, jnp.logical_not(first_tile)))
    def _():
        for ref in kv_refs:
            ref[:, 0:WINDOW, :] = ref[:, ROW_TILE:ROW_TILE + WINDOW, :]

    @pl.when(j == 0)
    def _():
        u = _rms_norm_f32(h_ref[...], gain_ref[...]).astype(_bf16)
        u_ref[...] = u
        acc_ref[...] = jnp.zeros_like(acc_ref)
        kv_all = _dot(u, wkv_ref[...])
        low = lax.broadcasted_iota(jnp.int32, (ROW_TILE, LANES), 1) < HEAD_DIM
        for g in range(N_KV_HEADS):
            kv = kv_all[:, g * LANES:(g + 1) * LANES]
            vk = pltpu.roll(kv, HEAD_DIM, axis=1)
            klo_ref[g, WINDOW:, :] = jnp.where(low, kv, 0.0).astype(_bf16)
            khi_ref[g, WINDOW:, :] = jnp.where(low, 0.0, vk).astype(_bf16)
            vlo_ref[g, WINDOW:, :] = jnp.where(low, vk, 0.0).astype(_bf16)
            vhi_ref[g, WINDOW:, :] = jnp.where(low, 0.0, kv).astype(_bf16)

    u = u_ref[...]
    q_ref[...] = (_dot(u, wq_ref[...]) * ATTN_SCALE).astype(_bf16)
    col = pl.multiple_of(j * GROUP_WIDTH, GROUP_WIDTH)
    gate_ref[:, pl.ds(col, GROUP_WIDTH)] = jax.nn.sigmoid(_dot(u, wga_ref[...]))

    qi = lax.broadcasted_iota(jnp.int32, (WINDOW, WINDOW), 0)
    kc = lax.broadcasted_iota(jnp.int32, (WINDOW, WINDOW), 1)
    in_cur = kc <= qi
    dist = (qi - kc + jnp.where(in_cur, 0, WINDOW)).astype(_f32)
    low_lane = kc < HEAD_DIM
    neg_inf = jnp.float32(-jnp.inf)
    sum_r = lax.broadcasted_iota(jnp.int32, (4 * WINDOW, LANES), 0) < 2 * WINDOW
    sum_c = lax.broadcasted_iota(jnp.int32, (4 * WINDOW, LANES), 1) < HEAD_DIM
    sum_cols = jnp.where(sum_r == sum_c, 1.0, 0.0).astype(_bf16)

    for blk in range(blocks):
        r0 = blk * WINDOW
        q_rows = jnp.concatenate(
            [q_ref[r0:r0 + WINDOW, pair * LANES:(pair + 1) * LANES] for pair in range(pairs)], axis=0)
        keys = jnp.concatenate([klo_ref[j, r0:r0 + 2 * WINDOW, :], khi_ref[j, r0:r0 + 2 * WINDOW, :]], axis=0)
        s_all = lax.dot_general(q_rows, keys, (((1,), (1,)), ((), ())), preferred_element_type=_f32)
        prob_rows = []
        sink_terms = []
        for pair in range(pairs):
            prob_cols = []
            sink_pair = []
            for half in range(2):
                head = j * Q_GROUP + 2 * pair + half
                c0 = half * 2 * WINDOW
                s_prev = s_all[pair * WINDOW:(pair + 1) * WINDOW, c0:c0 + WINDOW]
                s_cur = s_all[pair * WINDOW:(pair + 1) * WINDOW, c0 + WINDOW:c0 + 2 * WINDOW]
                if blk == 0:
                    s_prev = jnp.where(first_tile, neg_inf, s_prev)
                s = jnp.where(in_cur, s_cur, s_prev) - slopes_ref[head] * dist
                sink = sinks_ref[head]
                m = jnp.maximum(jnp.max(s, axis=-1, keepdims=True), sink)
                p = jnp.exp(s - m).astype(_bf16)
                zero = jnp.zeros_like(p)
                prob_cols += [jnp.where(in_cur, zero, p), jnp.where(in_cur, p, zero)]
                sink_pair.append(jnp.exp(sink - m))
            prob_rows.append(jnp.concatenate(prob_cols, axis=1))
            sink_terms.append(jnp.where(low_lane, sink_pair[0], sink_pair[1]))
        probs = jnp.concatenate(prob_rows, axis=0)
        vals = jnp.concatenate([vlo_ref[j, r0:r0 + 2 * WINDOW, :], vhi_ref[j, r0:r0 + 2 * WINDOW, :]], axis=0)
        o = _dot(probs, jnp.concatenate([vals, sum_cols], axis=1))
        for pair in range(pairs):
            rows = slice(pair * WINDOW, (pair + 1) * WINDOW)
            denom = o[rows, LANES:] + sink_terms[pair]
            a_ref[r0:r0 + WINDOW, pair * LANES:(pair + 1) * LANES] = (o[rows, :LANES] / denom).astype(_bf16)

    acc_ref[...] += _dot(a_ref[...], wao_ref[...])

    @pl.when(j == pl.num_programs(1) - 1)
    def _():
        merged = mconv_ref[...].astype(_f32) + gate_ref[...] * acc_ref[...]
        out_ref[...] = merged.astype(_bf16)


def _attn_branch(h, gain, mconv, w_in, wkv, wao, slopes, sinks, tiles_per_seq):
    n = h.shape[0]
    q_off = 3 * D_MODEL // GROUP_WIDTH
    ga_off = (D_MODEL * 5 + 2 * N_KV_HEADS * HEAD_DIM) // GROUP_WIDTH
    row = lambda i, j: (i, 0)
    const = lambda i, j: (0, 0)
    smem = pl.BlockSpec(memory_space=pltpu.SMEM)
    kv_scratch = pltpu.VMEM((N_KV_HEADS, ROW_TILE + WINDOW, LANES), _bf16)
    return pl.pallas_call(
        functools.partial(_attn_kernel, tiles_per_seq=tiles_per_seq),
        grid=(n // ROW_TILE, N_KV_HEADS),
        in_specs=[
            smem, smem,
            pl.BlockSpec((ROW_TILE, D_MODEL), row),
            pl.BlockSpec((1, D_MODEL), const),
            pl.BlockSpec((ROW_TILE, D_MODEL), row),
            pl.BlockSpec((D_MODEL, GROUP_WIDTH), lambda i, j: (0, j + q_off)),
            pl.BlockSpec((D_MODEL, N_KV_HEADS * LANES), const, pipeline_mode=pl.Buffered(1)),
            pl.BlockSpec((D_MODEL, GROUP_WIDTH), lambda i, j: (0, j + ga_off)),
            pl.BlockSpec((GROUP_WIDTH, D_MODEL), lambda i, j: (j, 0)),
        ],
        out_specs=pl.BlockSpec((ROW_TILE, D_MODEL), row),
        out_shape=jax.ShapeDtypeStruct((n, D_MODEL), _bf16),
        scratch_shapes=[
            pltpu.VMEM((ROW_TILE, D_MODEL), _bf16),
            pltpu.VMEM((ROW_TILE, D_MODEL), _f32),
            pltpu.VMEM((ROW_TILE, D_MODEL), _f32),
            pltpu.VMEM((ROW_TILE, GROUP_WIDTH), _bf16),
            pltpu.VMEM((ROW_TILE, GROUP_WIDTH), _bf16),
            kv_scratch, kv_scratch, kv_scratch, kv_scratch,
        ],
        compiler_params=pltpu.CompilerParams(
            dimension_semantics=("arbitrary", "arbitrary"),
            vmem_limit_bytes=VMEM_LIMIT_BYTES),
        name="attn_branch",
    )(slopes, sinks, h, gain, mconv, w_in, wkv, w_in, wao)


def kernel(x, norm_ffn1, w_gate1, w_up1, w_down1, norm_mix, w_in, w_conv, w_conv_out, attn_sinks,
           w_attn_out, w_out, norm_ffn2, w_gate2, w_up2, w_down2, norm_final):
    batch, seq, d = x.shape
    depth = norm_ffn1.shape[0]
    assert d == D_MODEL and seq % ROW_TILE == 0 and ROW_TILE % WINDOW == 0
    tiles_per_seq = seq // ROW_TILE
    bf = lambda w: w.astype(_bf16)
    gain = lambda g: g.reshape(1, D_MODEL).astype(_f32)
    slopes = 2.0 ** (-8.0 * jnp.arange(1, N_Q_HEADS + 1, dtype=_f32) / N_Q_HEADS)
    k_off = 4 * D_MODEL
    v_off = k_off + N_KV_HEADS * HEAD_DIM

    h = x.reshape(batch * seq, D_MODEL)
    gfin = gain(norm_final)
    for l in range(depth):
        last = l == depth - 1
        w_in_l = bf(w_in[l])
        wk = w_in_l[:, k_off:v_off].reshape(D_MODEL, N_KV_HEADS, HEAD_DIM)
        wv = w_in_l[:, v_off:v_off + N_KV_HEADS * HEAD_DIM].reshape(D_MODEL, N_KV_HEADS, HEAD_DIM)
        wkv = jnp.concatenate([wk, wv], axis=-1).reshape(D_MODEL, N_KV_HEADS * LANES)

        h = _ffn(h, gain(norm_ffn1[l]), bf(w_gate1[l]), bf(w_up1[l]), bf(w_down1[l]), gfin, False)
        g_mix = gain(norm_mix[l])
        mconv = _conv_branch(h, g_mix, w_in_l, w_conv[l].astype(_f32), bf(w_conv_out[l]), tiles_per_seq)
        merged = _attn_branch(h, g_mix, mconv, w_in_l, wkv, bf(w_attn_out[l]), slopes,
                              attn_sinks[l].astype(_f32), tiles_per_seq)
        h = _ffn(h, gain(norm_ffn2[l]), bf(w_gate2[l]), bf(w_up2[l]), bf(w_down2[l]), gfin, last,
                 merged=merged, w_out=bf(w_out[l]))
    return h.reshape(batch, seq, D_MODEL)
```

```python
import functools

import jax
import jax.numpy as jnp
from jax import lax
from jax.experimental import pallas as pl
from jax.experimental.pallas import tpu as pltpu

D_MODEL = 2048
D_FF = 5632
CONV_K = 3
N_Q_HEADS = 32
N_KV_HEADS = 4
HEAD_DIM = 64
Q_GROUP = N_Q_HEADS // N_KV_HEADS
GROUP_WIDTH = Q_GROUP * HEAD_DIM
WINDOW = 128
RMS_EPS = 1e-5
ATTN_SCALE = HEAD_DIM ** -0.5

LANES = 128
SUBLANES = 8
VMEM_LIMIT_BYTES = 60 * 1024 * 1024

ROW_TILE = 512
FFN_ROW_TILE = 1024
FF_TILE = 512
NORM_ROWS = 128
CONV_TILE = 512

_f32 = jnp.float32
_bf16 = jnp.bfloat16


def _rms_norm_f32(x, gain):
    return x * lax.rsqrt(jnp.mean(x * x, axis=-1, keepdims=True) + RMS_EPS) * gain


def _dot(a, b):
    return jnp.dot(a, b, preferred_element_type=_f32)


def _ffn_kernel(h_ref, gain_ref, wg_ref, wu_ref, wd_ref, gfin_ref, out_ref, u_ref, *, final_norm):
    j = pl.program_id(1)

    def row_chunks(body, unroll):
        def step(c, carry):
            body(pl.ds(pl.multiple_of(c * NORM_ROWS, NORM_ROWS), NORM_ROWS))
            return carry
        lax.fori_loop(0, FFN_ROW_TILE // NORM_ROWS, step, 0, unroll=unroll)

    @pl.when(j == 0)
    def _():
        def body(rows):
            out_ref[rows, :] = jnp.zeros((NORM_ROWS, D_MODEL), _f32)
            u_ref[rows, :] = _rms_norm_f32(h_ref[rows, :], gain_ref[...]).astype(_bf16)
        row_chunks(body, unroll=2)

    u = u_ref[...]
    halves = []
    for c0 in range(0, FF_TILE, FF_TILE // 2):
        g = _dot(u, wg_ref[:, c0:c0 + FF_TILE // 2])
        up = _dot(u, wu_ref[:, c0:c0 + FF_TILE // 2])
        halves.append((g * jax.nn.sigmoid(g) * up).astype(_bf16))
    out_ref[...] += _dot(jnp.concatenate(halves, axis=1), wd_ref[...])

    @pl.when(j == pl.num_programs(1) - 1)
    def _():
        def body(rows):
            h = h_ref[rows, :] + 0.5 * out_ref[rows, :]
            if final_norm:
                h = _rms_norm_f32(h, gfin_ref[...])
            out_ref[rows, :] = h
        row_chunks(body, unroll=1)


def _ffn(h, gain, wg, wu, wd, gfin, final_norm):
    n = h.shape[0]
    row = lambda i, j: (i, 0)
    const = lambda i, j: (0, 0)
    return pl.pallas_call(
        functools.partial(_ffn_kernel, final_norm=final_norm),
        grid=(n // FFN_ROW_TILE, D_FF // FF_TILE),
        in_specs=[
            pl.BlockSpec((FFN_ROW_TILE, D_MODEL), row),
            pl.BlockSpec((1, D_MODEL), const),
            pl.BlockSpec((D_MODEL, FF_TILE), lambda i, j: (0, j)),
            pl.BlockSpec((D_MODEL, FF_TILE), lambda i, j: (0, j)),
            pl.BlockSpec((FF_TILE, D_MODEL), lambda i, j: (j, 0)),
            pl.BlockSpec((1, D_MODEL), const),
        ],
        out_specs=pl.BlockSpec((FFN_ROW_TILE, D_MODEL), row),
        out_shape=jax.ShapeDtypeStruct((n, D_MODEL), _f32),
        scratch_shapes=[pltpu.VMEM((FFN_ROW_TILE, D_MODEL), _bf16)],
        compiler_params=pltpu.CompilerParams(
            dimension_semantics=("arbitrary", "arbitrary"),
            vmem_limit_bytes=VMEM_LIMIT_BYTES),
        name="ffn_final" if final_norm else "ffn",
    )(h, gain, wg, wu, wd, gfin)


def _conv_kernel(h_ref, gain_ref, wb_ref, wc_ref, wx_ref, wgc_ref, taps_ref, wco_ref, out_ref, u_ref,
                 acc_ref, gate_ref, ucbuf_ref, carry_ref, *, tiles_per_seq):
    i = pl.program_id(0)
    j = pl.program_id(1)

    @pl.when(j == 0)
    def _():
        u_ref[...] = _rms_norm_f32(h_ref[...], gain_ref[...]).astype(_bf16)
        acc_ref[...] = jnp.zeros_like(acc_ref)

    @pl.when(i % tiles_per_seq == 0)
    def _():
        carry_ref[j] = jnp.zeros((SUBLANES, CONV_TILE), _f32)

    u = u_ref[...]
    uc = _dot(u, wc_ref[...]) * _dot(u, wx_ref[...])
    ucbuf_ref[0:SUBLANES, :] = carry_ref[j]
    ucbuf_ref[SUBLANES:, :] = uc
    carry_ref[j] = uc[ROW_TILE - SUBLANES:, :]
    b = _dot(u, wb_ref[...])
    col = pl.multiple_of(j * CONV_TILE, CONV_TILE)
    gate_ref[:, pl.ds(col, CONV_TILE)] = jax.nn.sigmoid(_dot(u, wgc_ref[...]))
    taps = taps_ref[...]
    conv = taps[CONV_K - 1:CONV_K, :] * uc
    for tap in range(CONV_K - 1):
        back = CONV_K - 1 - tap
        conv = conv + taps[tap:tap + 1, :] * ucbuf_ref[SUBLANES - back:SUBLANES - back + ROW_TILE, :]
    acc_ref[...] += _dot((b * conv).astype(_bf16), wco_ref[...])

    @pl.when(j == pl.num_programs(1) - 1)
    def _():
        out_ref[...] = (gate_ref[...] * acc_ref[...]).astype(_bf16)


def _conv_branch(h, gain, w_in, taps, wco, tiles_per_seq):
    n = h.shape[0]
    nblk = D_MODEL // CONV_TILE
    row = lambda i, j: (i, 0)
    const = lambda i, j: (0, 0)
    gc_off = (4 * D_MODEL + 2 * N_KV_HEADS * HEAD_DIM) // CONV_TILE
    wcol = lambda blk_off: (lambda i, j: (0, j + blk_off))
    return pl.pallas_call(
        functools.partial(_conv_kernel, tiles_per_seq=tiles_per_seq),
        grid=(n // ROW_TILE, nblk),
        in_specs=[
            pl.BlockSpec((ROW_TILE, D_MODEL), row),
            pl.BlockSpec((1, D_MODEL), const),
            pl.BlockSpec((D_MODEL, CONV_TILE), wcol(0)),
            pl.BlockSpec((D_MODEL, CONV_TILE), wcol(nblk)),
            pl.BlockSpec((D_MODEL, CONV_TILE), wcol(2 * nblk)),
            pl.BlockSpec((D_MODEL, CONV_TILE), wcol(gc_off)),
            pl.BlockSpec((CONV_K, CONV_TILE), lambda i, j: (0, j)),
            pl.BlockSpec((CONV_TILE, D_MODEL), lambda i, j: (j, 0)),
        ],
        out_specs=[pl.BlockSpec((ROW_TILE, D_MODEL), row), pl.BlockSpec((ROW_TILE, D_MODEL), row)],
        out_shape=[jax.ShapeDtypeStruct((n, D_MODEL), _bf16), jax.ShapeDtypeStruct((n, D_MODEL), _bf16)],
        scratch_shapes=[
            pltpu.VMEM((ROW_TILE, D_MODEL), _f32),
            pltpu.VMEM((ROW_TILE, D_MODEL), _f32),
            pltpu.VMEM((ROW_TILE + SUBLANES, CONV_TILE), _f32),
            pltpu.VMEM((nblk, SUBLANES, CONV_TILE), _f32),
        ],
        compiler_params=pltpu.CompilerParams(
            dimension_semantics=("arbitrary", "arbitrary"),
            vmem_limit_bytes=VMEM_LIMIT_BYTES),
        name="conv_branch",
    )(h, gain, w_in, w_in, w_in, w_in, taps, wco)


def _attn_kernel(slopes_ref, sinks_ref, u_ref, mconv_ref, wq_ref, wkv_ref, wga_ref, wao_ref,
                 out_ref, acc_ref, gate_ref, q_ref, a_ref, klo_ref, khi_ref, vlo_ref, vhi_ref,
                 *, tiles_per_seq):
    i = pl.program_id(0)
    j = pl.program_id(1)
    blocks = ROW_TILE // WINDOW
    pairs = Q_GROUP // 2
    first_tile = i % tiles_per_seq == 0
    kv_refs = (klo_ref, khi_ref, vlo_ref, vhi_ref)

    @pl.when(jnp.logical_and(j == 0, first_tile))
    def _():
        for ref in kv_refs:
            ref[:, 0:WINDOW, :] = jnp.zeros((N_KV_HEADS, WINDOW, LANES), _bf16)

    @pl.when(jnp.logical_and(j == 0, jnp.logical_not(first_tile)))
    def _():
        for ref in kv_refs:
            ref[:, 0:WINDOW, :] = ref[:, ROW_TILE:ROW_TILE + WINDOW, :]

    @pl.when(j == 0)
    def _():
        acc_ref[...] = jnp.zeros_like(acc_ref)
        kv_all = _dot(u_ref[...], wkv_ref[...])
        low = lax.broadcasted_iota(jnp.int32, (ROW_TILE, LANES), 1) < HEAD_DIM
        for g in range(N_KV_HEADS):
            kv = kv_all[:, g * LANES:(g + 1) * LANES]
            vk = pltpu.roll(kv, HEAD_DIM, axis=1)
            klo_ref[g, WINDOW:, :] = jnp.where(low, kv, 0.0).astype(_bf16)
            khi_ref[g, WINDOW:, :] = jnp.where(low, 0.0, vk).astype(_bf16)
            vlo_ref[g, WINDOW:, :] = jnp.where(low, vk, 0.0).astype(_bf16)
            vhi_ref[g, WINDOW:, :] = jnp.where(low, 0.0, kv).astype(_bf16)

    u = u_ref[...]
    q_ref[...] = (_dot(u, wq_ref[...]) * ATTN_SCALE).astype(_bf16)
    col = pl.multiple_of(j * GROUP_WIDTH, GROUP_WIDTH)
    gate_ref[:, pl.ds(col, GROUP_WIDTH)] = jax.nn.sigmoid(_dot(u, wga_ref[...]))

    qi = lax.broadcasted_iota(jnp.int32, (WINDOW, WINDOW), 0)
    kc = lax.broadcasted_iota(jnp.int32, (WINDOW, WINDOW), 1)
    in_cur = kc <= qi
    dist = (qi - kc + jnp.where(in_cur, 0, WINDOW)).astype(_f32)
    low_lane = kc < HEAD_DIM
    neg_inf = jnp.float32(-jnp.inf)
    sum_r = lax.broadcasted_iota(jnp.int32, (4 * WINDOW, LANES), 0) < 2 * WINDOW
    sum_c = lax.broadcasted_iota(jnp.int32, (4 * WINDOW, LANES), 1) < HEAD_DIM
    sum_cols = jnp.where(sum_r == sum_c, 1.0, 0.0).astype(_bf16)

    for blk in range(blocks):
        r0 = blk * WINDOW
        q_rows = jnp.concatenate(
            [q_ref[r0:r0 + WINDOW, pair * LANES:(pair + 1) * LANES] for pair in range(pairs)], axis=0)
        keys = jnp.concatenate([klo_ref[j, r0:r0 + 2 * WINDOW, :], khi_ref[j, r0:r0 + 2 * WINDOW, :]], axis=0)
        s_all = lax.dot_general(q_rows, keys, (((1,), (1,)), ((), ())), preferred_element_type=_f32)
        prob_rows = []
        sink_terms = []
        for pair in range(pairs):
            prob_cols = []
            sink_pair = []
            for half in range(2):
                head = j * Q_GROUP + 2 * pair + half
                c0 = half * 2 * WINDOW
                s_prev = s_all[pair * WINDOW:(pair + 1) * WINDOW, c0:c0 + WINDOW]
                s_cur = s_all[pair * WINDOW:(pair + 1) * WINDOW, c0 + WINDOW:c0 + 2 * WINDOW]
                if blk == 0:
                    s_prev = jnp.where(first_tile, neg_inf, s_prev)
                s = jnp.where(in_cur, s_cur, s_prev) - slopes_ref[head] * dist
                sink = sinks_ref[head]
                m = jnp.maximum(jnp.max(s, axis=-1, keepdims=True), sink)
                p = jnp.exp(s - m).astype(_bf16)
                zero = jnp.zeros_like(p)
                prob_cols += [jnp.where(in_cur, zero, p), jnp.where(in_cur, p, zero)]
                sink_pair.append(jnp.exp(sink - m))
            prob_rows.append(jnp.concatenate(prob_cols, axis=1))
            sink_terms.append(jnp.where(low_lane, sink_pair[0], sink_pair[1]))
        probs = jnp.concatenate(prob_rows, axis=0)
        vals = jnp.concatenate([vlo_ref[j, r0:r0 + 2 * WINDOW, :], vhi_ref[j, r0:r0 + 2 * WINDOW, :]], axis=0)
        o = _dot(probs, jnp.concatenate([vals, sum_cols], axis=1))
        for pair in range(pairs):
            rows = slice(pair * WINDOW, (pair + 1) * WINDOW)
            denom = o[rows, LANES:] + sink_terms[pair]
            a_ref[r0:r0 + WINDOW, pair * LANES:(pair + 1) * LANES] = (o[rows, :LANES] / denom).astype(_bf16)

    acc_ref[...] += _dot(a_ref[...], wao_ref[...])

    @pl.when(j == pl.num_programs(1) - 1)
    def _():
        merged = mconv_ref[...].astype(_f32) + gate_ref[...] * acc_ref[...]
        out_ref[...] = merged.astype(_bf16)


def _attn_branch(u, mconv, w_in, wkv, wao, slopes, sinks, tiles_per_seq):
    n = u.shape[0]
    q_off = 3 * D_MODEL // GROUP_WIDTH
    ga_off = (D_MODEL * 5 + 2 * N_KV_HEADS * HEAD_DIM) // GROUP_WIDTH
    row = lambda i, j: (i, 0)
    const = lambda i, j: (0, 0)
    smem = pl.BlockSpec(memory_space=pltpu.SMEM)
    kv_scratch = pltpu.VMEM((N_KV_HEADS, ROW_TILE + WINDOW, LANES), _bf16)
    return pl.pallas_call(
        functools.partial(_attn_kernel, tiles_per_seq=tiles_per_seq),
        grid=(n // ROW_TILE, N_KV_HEADS),
        in_specs=[
            smem, smem,
            pl.BlockSpec((ROW_TILE, D_MODEL), row),
            pl.BlockSpec((ROW_TILE, D_MODEL), row),
            pl.BlockSpec((D_MODEL, GROUP_WIDTH), lambda i, j: (0, j + q_off)),
            pl.BlockSpec((D_MODEL, N_KV_HEADS * LANES), const, pipeline_mode=pl.Buffered(1)),
            pl.BlockSpec((D_MODEL, GROUP_WIDTH), lambda i, j: (0, j + ga_off)),
            pl.BlockSpec((GROUP_WIDTH, D_MODEL), lambda i, j: (j, 0)),
        ],
        out_specs=pl.BlockSpec((ROW_TILE, D_MODEL), row),
        out_shape=jax.ShapeDtypeStruct((n, D_MODEL), _bf16),
        scratch_shapes=[
            pltpu.VMEM((ROW_TILE, D_MODEL), _f32),
            pltpu.VMEM((ROW_TILE, D_MODEL), _f32),
            pltpu.VMEM((ROW_TILE, GROUP_WIDTH), _bf16),
            pltpu.VMEM((ROW_TILE, GROUP_WIDTH), _bf16),
            kv_scratch, kv_scratch, kv_scratch, kv_scratch,
        ],
        compiler_params=pltpu.CompilerParams(
            dimension_semantics=("arbitrary", "arbitrary"),
            vmem_limit_bytes=VMEM_LIMIT_BYTES),
        name="attn_branch",
    )(slopes, sinks, u, mconv, w_in, wkv, w_in, wao)


def _oproj_kernel(h_ref, m_ref, w_ref, out_ref):
    out_ref[...] = h_ref[...] + _dot(m_ref[...], w_ref[...])


def _oproj(h, merged, w_out):
    n = h.shape[0]
    row = lambda i: (i, 0)
    return pl.pallas_call(
        _oproj_kernel,
        grid=(n // ROW_TILE,),
        in_specs=[
            pl.BlockSpec((ROW_TILE, D_MODEL), row),
            pl.BlockSpec((ROW_TILE, D_MODEL), row),
            pl.BlockSpec((D_MODEL, D_MODEL), lambda i: (0, 0), pipeline_mode=pl.Buffered(1)),
        ],
        out_specs=pl.BlockSpec((ROW_TILE, D_MODEL), row),
        out_shape=jax.ShapeDtypeStruct((n, D_MODEL), _f32),
        compiler_params=pltpu.CompilerParams(
            dimension_semantics=("arbitrary",),
            vmem_limit_bytes=VMEM_LIMIT_BYTES),
        name="oproj",
    )(h, merged, w_out)


def kernel(x, norm_ffn1, w_gate1, w_up1, w_down1, norm_mix, w_in, w_conv, w_conv_out, attn_sinks,
           w_attn_out, w_out, norm_ffn2, w_gate2, w_up2, w_down2, norm_final):
    batch, seq, d = x.shape
    depth = norm_ffn1.shape[0]
    assert d == D_MODEL and seq % ROW_TILE == 0 and ROW_TILE % WINDOW == 0
    tiles_per_seq = seq // ROW_TILE
    bf = lambda w: w.astype(_bf16)
    gain = lambda g: g.reshape(1, D_MODEL).astype(_f32)
    slopes = 2.0 ** (-8.0 * jnp.arange(1, N_Q_HEADS + 1, dtype=_f32) / N_Q_HEADS)
    k_off = 4 * D_MODEL
    v_off = k_off + N_KV_HEADS * HEAD_DIM

    h = x.reshape(batch * seq, D_MODEL)
    gfin = gain(norm_final)
    for l in range(depth):
        last = l == depth - 1
        w_in_l = bf(w_in[l])
        wk = w_in_l[:, k_off:v_off].reshape(D_MODEL, N_KV_HEADS, HEAD_DIM)
        wv = w_in_l[:, v_off:v_off + N_KV_HEADS * HEAD_DIM].reshape(D_MODEL, N_KV_HEADS, HEAD_DIM)
        wkv = jnp.concatenate([wk, wv], axis=-1).reshape(D_MODEL, N_KV_HEADS * LANES)

        h = _ffn(h, gain(norm_ffn1[l]), bf(w_gate1[l]), bf(w_up1[l]), bf(w_down1[l]), gfin, False)
        g_mix = gain(norm_mix[l])
        mconv, u_mix = _conv_branch(h, g_mix, w_in_l, w_conv[l].astype(_f32), bf(w_conv_out[l]), tiles_per_seq)
        merged = _attn_branch(u_mix, mconv, w_in_l, wkv, bf(w_attn_out[l]), slopes,
                              attn_sinks[l].astype(_f32), tiles_per_seq)
        h = _oproj(h, merged, bf(w_out[l]))
        h = _ffn(h, gain(norm_ffn2[l]), bf(w_gate2[l]), bf(w_up2[l]), bf(w_down2[l]), gfin, last)
    return h.reshape(batch, seq, D_MODEL)
```

```python
import functools

import jax
import jax.numpy as jnp
from jax import lax
from jax.experimental import pallas as pl
from jax.experimental.pallas import tpu as pltpu

D_MODEL = 2048
D_FF = 5632
CONV_K = 3
N_Q_HEADS = 32
N_KV_HEADS = 4
HEAD_DIM = 64
Q_GROUP = N_Q_HEADS // N_KV_HEADS
GROUP_WIDTH = Q_GROUP * HEAD_DIM
WINDOW = 128
RMS_EPS = 1e-5
ATTN_SCALE = HEAD_DIM ** -0.5

LANES = 128
SUBLANES = 8
VMEM_LIMIT_BYTES = 60 * 1024 * 1024

ROW_TILE = 512
FFN_ROW_TILE = 1024
FF_TILE = 512
NORM_ROWS = 128
CONV_TILE = 512

_f32 = jnp.float32
_bf16 = jnp.bfloat16


def _rms_norm_f32(x, gain):
    return x * lax.rsqrt(jnp.mean(x * x, axis=-1, keepdims=True) + RMS_EPS) * gain


def _dot(a, b):
    return jnp.dot(a, b, preferred_element_type=_f32)


def _col_blocks(w, width):
    rows, cols = w.shape
    return w.astype(_bf16).reshape(rows, cols // width, width).transpose(1, 0, 2)


def _ffn_kernel(h_ref, gain_ref, wg_ref, wu_ref, wd_ref, gfin_ref, out_ref, u_ref, *, final_norm):
    j = pl.program_id(1)

    def row_chunks(body, unroll):
        def step(c, carry):
            body(pl.ds(pl.multiple_of(c * NORM_ROWS, NORM_ROWS), NORM_ROWS))
            return carry
        lax.fori_loop(0, FFN_ROW_TILE // NORM_ROWS, step, 0, unroll=unroll)

    @pl.when(j == 0)
    def _():
        def body(rows):
            out_ref[rows, :] = jnp.zeros((NORM_ROWS, D_MODEL), _f32)
            u_ref[rows, :] = _rms_norm_f32(h_ref[rows, :], gain_ref[...]).astype(_bf16)
        row_chunks(body, unroll=2)

    u = u_ref[...]
    halves = []
    for c0 in range(0, FF_TILE, FF_TILE // 2):
        g = _dot(u, wg_ref[:, c0:c0 + FF_TILE // 2])
        up = _dot(u, wu_ref[:, c0:c0 + FF_TILE // 2])
        halves.append((g * jax.nn.sigmoid(g) * up).astype(_bf16))
    out_ref[...] += _dot(jnp.concatenate(halves, axis=1), wd_ref[...])

    @pl.when(j == pl.num_programs(1) - 1)
    def _():
        def body(rows):
            h = h_ref[rows, :] + 0.5 * out_ref[rows, :]
            if final_norm:
                h = _rms_norm_f32(h, gfin_ref[...])
            out_ref[rows, :] = h
        row_chunks(body, unroll=1)


def _ffn(h, gain, wg, wu, wd, gfin, final_norm):
    n = h.shape[0]
    row = lambda i, j: (i, 0)
    const = lambda i, j: (0, 0)
    return pl.pallas_call(
        functools.partial(_ffn_kernel, final_norm=final_norm),
        grid=(n // FFN_ROW_TILE, D_FF // FF_TILE),
        in_specs=[
            pl.BlockSpec((FFN_ROW_TILE, D_MODEL), row),
            pl.BlockSpec((1, D_MODEL), const),
            pl.BlockSpec((None, D_MODEL, FF_TILE), lambda i, j: (j, 0, 0)),
            pl.BlockSpec((None, D_MODEL, FF_TILE), lambda i, j: (j, 0, 0)),
            pl.BlockSpec((FF_TILE, D_MODEL), lambda i, j: (j, 0)),
            pl.BlockSpec((1, D_MODEL), const),
        ],
        out_specs=pl.BlockSpec((FFN_ROW_TILE, D_MODEL), row),
        out_shape=jax.ShapeDtypeStruct((n, D_MODEL), _f32),
        scratch_shapes=[pltpu.VMEM((FFN_ROW_TILE, D_MODEL), _bf16)],
        compiler_params=pltpu.CompilerParams(
            dimension_semantics=("arbitrary", "arbitrary"),
            vmem_limit_bytes=VMEM_LIMIT_BYTES),
        name="ffn_final" if final_norm else "ffn",
    )(h, gain, wg, wu, wd, gfin)


def _conv_kernel(h_ref, gain_ref, wb_ref, wc_ref, wx_ref, wgc_ref, taps_ref, wco_ref, out_ref, u_ref,
                 acc_ref, gate_ref, ucbuf_ref, carry_ref, *, tiles_per_seq):
    i = pl.program_id(0)
    j = pl.program_id(1)

    @pl.when(j == 0)
    def _():
        u_ref[...] = _rms_norm_f32(h_ref[...], gain_ref[...]).astype(_bf16)
        acc_ref[...] = jnp.zeros_like(acc_ref)

    @pl.when(i % tiles_per_seq == 0)
    def _():
        carry_ref[j] = jnp.zeros((SUBLANES, CONV_TILE), _f32)

    u = u_ref[...]
    uc = _dot(u, wc_ref[...]) * _dot(u, wx_ref[...])
    ucbuf_ref[0:SUBLANES, :] = carry_ref[j]
    ucbuf_ref[SUBLANES:, :] = uc
    carry_ref[j] = uc[ROW_TILE - SUBLANES:, :]
    b = _dot(u, wb_ref[...])
    col = pl.multiple_of(j * CONV_TILE, CONV_TILE)
    gate_ref[:, pl.ds(col, CONV_TILE)] = jax.nn.sigmoid(_dot(u, wgc_ref[...]))
    taps = taps_ref[...]
    conv = taps[CONV_K - 1:CONV_K, :] * uc
    for tap in range(CONV_K - 1):
        back = CONV_K - 1 - tap
        conv = conv + taps[tap:tap + 1, :] * ucbuf_ref[SUBLANES - back:SUBLANES - back + ROW_TILE, :]
    acc_ref[...] += _dot((b * conv).astype(_bf16), wco_ref[...])

    @pl.when(j == pl.num_programs(1) - 1)
    def _():
        out_ref[...] = (gate_ref[...] * acc_ref[...]).astype(_bf16)


def _conv_branch(h, gain, w_in, taps, wco, tiles_per_seq):
    n = h.shape[0]
    nblk = D_MODEL // CONV_TILE
    row = lambda i, j: (i, 0)
    const = lambda i, j: (0, 0)
    gc_off = (4 * D_MODEL + 2 * N_KV_HEADS * HEAD_DIM) // CONV_TILE
    wcol = lambda blk_off: (lambda i, j: (j + blk_off, 0, 0))
    return pl.pallas_call(
        functools.partial(_conv_kernel, tiles_per_seq=tiles_per_seq),
        grid=(n // ROW_TILE, nblk),
        in_specs=[
            pl.BlockSpec((ROW_TILE, D_MODEL), row),
            pl.BlockSpec((1, D_MODEL), const),
            pl.BlockSpec((None, D_MODEL, CONV_TILE), wcol(0)),
            pl.BlockSpec((None, D_MODEL, CONV_TILE), wcol(nblk)),
            pl.BlockSpec((None, D_MODEL, CONV_TILE), wcol(2 * nblk)),
            pl.BlockSpec((None, D_MODEL, CONV_TILE), wcol(gc_off)),
            pl.BlockSpec((CONV_K, CONV_TILE), lambda i, j: (0, j)),
            pl.BlockSpec((CONV_TILE, D_MODEL), lambda i, j: (j, 0)),
        ],
        out_specs=[pl.BlockSpec((ROW_TILE, D_MODEL), row), pl.BlockSpec((ROW_TILE, D_MODEL), row)],
        out_shape=[jax.ShapeDtypeStruct((n, D_MODEL), _bf16), jax.ShapeDtypeStruct((n, D_MODEL), _bf16)],
        scratch_shapes=[
            pltpu.VMEM((ROW_TILE, D_MODEL), _f32),
            pltpu.VMEM((ROW_TILE, D_MODEL), _f32),
            pltpu.VMEM((ROW_TILE + SUBLANES, CONV_TILE), _f32),
            pltpu.VMEM((nblk, SUBLANES, CONV_TILE), _f32),
        ],
        compiler_params=pltpu.CompilerParams(
            dimension_semantics=("arbitrary", "arbitrary"),
            vmem_limit_bytes=VMEM_LIMIT_BYTES),
        name="conv_branch",
    )(h, gain, w_in, w_in, w_in, w_in, taps, wco)


def _attn_kernel(slopes_ref, sinks_ref, u_ref, mconv_ref, wq_ref, wkv_ref, wga_ref, wao_ref,
                 out_ref, acc_ref, gate_ref, q_ref, a_ref, klo_ref, khi_ref, vlo_ref, vhi_ref,
                 *, tiles_per_seq):
    i = pl.program_id(0)
    j = pl.program_id(1)
    blocks = ROW_TILE // WINDOW
    pairs = Q_GROUP // 2
    first_tile = i % tiles_per_seq == 0
    kv_refs = (klo_ref, khi_ref, vlo_ref, vhi_ref)

    @pl.when(jnp.logical_and(j == 0, first_tile))
    def _():
        for ref in kv_refs:
            ref[:, 0:WINDOW, :] = jnp.zeros((N_KV_HEADS, WINDOW, LANES), _bf16)

    @pl.when(jnp.logical_and(j == 0, jnp.logical_not(first_tile)))
    def _():
        for ref in kv_refs:
            ref[:, 0:WINDOW, :] = ref[:, ROW_TILE:ROW_TILE + WINDOW, :]

    @pl.when(j == 0)
    def _():
        acc_ref[...] = jnp.zeros_like(acc_ref)
        kv_all = _dot(u_ref[...], wkv_ref[...])
        low = lax.broadcasted_iota(jnp.int32, (ROW_TILE, LANES), 1) < HEAD_DIM
        for g in range(N_KV_HEADS):
            kv = kv_all[:, g * LANES:(g + 1) * LANES]
            vk = pltpu.roll(kv, HEAD_DIM, axis=1)
            klo_ref[g, WINDOW:, :] = jnp.where(low, kv, 0.0).astype(_bf16)
            khi_ref[g, WINDOW:, :] = jnp.where(low, 0.0, vk).astype(_bf16)
            vlo_ref[g, WINDOW:, :] = jnp.where(low, vk, 0.0).astype(_bf16)
            vhi_ref[g, WINDOW:, :] = jnp.where(low, 0.0, kv).astype(_bf16)

    u = u_ref[...]
    q_ref[...] = (_dot(u, wq_ref[...]) * ATTN_SCALE).astype(_bf16)
    col = pl.multiple_of(j * GROUP_WIDTH, GROUP_WIDTH)
    gate_ref[:, pl.ds(col, GROUP_WIDTH)] = jax.nn.sigmoid(_dot(u, wga_ref[...]))

    qi = lax.broadcasted_iota(jnp.int32, (WINDOW, WINDOW), 0)
    kc = lax.broadcasted_iota(jnp.int32, (WINDOW, WINDOW), 1)
    in_cur = kc <= qi
    dist = (qi - kc + jnp.where(in_cur, 0, WINDOW)).astype(_f32)
    low_lane = kc < HEAD_DIM
    neg_inf = jnp.float32(-jnp.inf)
    sum_r = lax.broadcasted_iota(jnp.int32, (4 * WINDOW, LANES), 0) < 2 * WINDOW
    sum_c = lax.broadcasted_iota(jnp.int32, (4 * WINDOW, LANES), 1) < HEAD_DIM
    sum_cols = jnp.where(sum_r == sum_c, 1.0, 0.0).astype(_bf16)

    for blk in range(blocks):
        r0 = blk * WINDOW
        q_rows = jnp.concatenate(
            [q_ref[r0:r0 + WINDOW, pair * LANES:(pair + 1) * LANES] for pair in range(pairs)], axis=0)
        keys = jnp.concatenate([klo_ref[j, r0:r0 + 2 * WINDOW, :], khi_ref[j, r0:r0 + 2 * WINDOW, :]], axis=0)
        s_all = lax.dot_general(q_rows, keys, (((1,), (1,)), ((), ())), preferred_element_type=_f32)
        prob_rows = []
        sink_terms = []
        for pair in range(pairs):
            prob_cols = []
            sink_pair = []
            for half in range(2):
                head = j * Q_GROUP + 2 * pair + half
                c0 = half * 2 * WINDOW
                s_prev = s_all[pair * WINDOW:(pair + 1) * WINDOW, c0:c0 + WINDOW]
                s_cur = s_all[pair * WINDOW:(pair + 1) * WINDOW, c0 + WINDOW:c0 + 2 * WINDOW]
                if blk == 0:
                    s_prev = jnp.where(first_tile, neg_inf, s_prev)
                s = jnp.where(in_cur, s_cur, s_prev) - slopes_ref[head] * dist
                sink = sinks_ref[head]
                m = jnp.maximum(jnp.max(s, axis=-1, keepdims=True), sink)
                p = jnp.exp(s - m).astype(_bf16)
                zero = jnp.zeros_like(p)
                prob_cols += [jnp.where(in_cur, zero, p), jnp.where(in_cur, p, zero)]
                sink_pair.append(jnp.exp(sink - m))
            prob_rows.append(jnp.concatenate(prob_cols, axis=1))
            sink_terms.append(jnp.where(low_lane, sink_pair[0], sink_pair[1]))
        probs = jnp.concatenate(prob_rows, axis=0)
        vals = jnp.concatenate([vlo_ref[j, r0:r0 + 2 * WINDOW, :], vhi_ref[j, r0:r0 + 2 * WINDOW, :]], axis=0)
        o = _dot(probs, jnp.concatenate([vals, sum_cols], axis=1))
        for pair in range(pairs):
            rows = slice(pair * WINDOW, (pair + 1) * WINDOW)
            denom = o[rows, LANES:] + sink_terms[pair]
            a_ref[r0:r0 + WINDOW, pair * LANES:(pair + 1) * LANES] = (o[rows, :LANES] / denom).astype(_bf16)

    acc_ref[...] += _dot(a_ref[...], wao_ref[...])

    @pl.when(j == pl.num_programs(1) - 1)
    def _():
        merged = mconv_ref[...].astype(_f32) + gate_ref[...] * acc_ref[...]
        out_ref[...] = merged.astype(_bf16)


def _attn_branch(u, mconv, w_in, wkv, wao, slopes, sinks, tiles_per_seq):
    n = u.shape[0]
    q_off = 3 * D_MODEL // GROUP_WIDTH
    ga_off = (D_MODEL * 5 + 2 * N_KV_HEADS * HEAD_DIM) // GROUP_WIDTH
    row = lambda i, j: (i, 0)
    const = lambda i, j: (0, 0)
    smem = pl.BlockSpec(memory_space=pltpu.SMEM)
    kv_scratch = pltpu.VMEM((N_KV_HEADS, ROW_TILE + WINDOW, LANES), _bf16)
    return pl.pallas_call(
        functools.partial(_attn_kernel, tiles_per_seq=tiles_per_seq),
        grid=(n // ROW_TILE, N_KV_HEADS),
        in_specs=[
            smem, smem,
            pl.BlockSpec((ROW_TILE, D_MODEL), row),
            pl.BlockSpec((ROW_TILE, D_MODEL), row),
            pl.BlockSpec((None, D_MODEL, GROUP_WIDTH), lambda i, j: (j + q_off, 0, 0)),
            pl.BlockSpec((D_MODEL, N_KV_HEADS * LANES), const, pipeline_mode=pl.Buffered(1)),
            pl.BlockSpec((None, D_MODEL, GROUP_WIDTH), lambda i, j: (j + ga_off, 0, 0)),
            pl.BlockSpec((GROUP_WIDTH, D_MODEL), lambda i, j: (j, 0)),
        ],
        out_specs=pl.BlockSpec((ROW_TILE, D_MODEL), row),
        out_shape=jax.ShapeDtypeStruct((n, D_MODEL), _bf16),
        scratch_shapes=[
            pltpu.VMEM((ROW_TILE, D_MODEL), _f32),
            pltpu.VMEM((ROW_TILE, D_MODEL), _f32),
            pltpu.VMEM((ROW_TILE, GROUP_WIDTH), _bf16),
            pltpu.VMEM((ROW_TILE, GROUP_WIDTH), _bf16),
            kv_scratch, kv_scratch, kv_scratch, kv_scratch,
        ],
        compiler_params=pltpu.CompilerParams(
            dimension_semantics=("arbitrary", "arbitrary"),
            vmem_limit_bytes=VMEM_LIMIT_BYTES),
        name="attn_branch",
    )(slopes, sinks, u, mconv, w_in, wkv, w_in, wao)


def _oproj_kernel(h_ref, m_ref, w_ref, out_ref):
    out_ref[...] = h_ref[...] + _dot(m_ref[...], w_ref[...])


def _oproj(h, merged, w_out):
    n = h.shape[0]
    row = lambda i: (i, 0)
    return pl.pallas_call(
        _oproj_kernel,
        grid=(n // ROW_TILE,),
        in_specs=[
            pl.BlockSpec((ROW_TILE, D_MODEL), row),
            pl.BlockSpec((ROW_TILE, D_MODEL), row),
            pl.BlockSpec((D_MODEL, D_MODEL), lambda i: (0, 0), pipeline_mode=pl.Buffered(1)),
        ],
        out_specs=pl.BlockSpec((ROW_TILE, D_MODEL), row),
        out_shape=jax.ShapeDtypeStruct((n, D_MODEL), _f32),
        compiler_params=pltpu.CompilerParams(
            dimension_semantics=("arbitrary",),
            vmem_limit_bytes=VMEM_LIMIT_BYTES),
        name="oproj",
    )(h, merged, w_out)


def kernel(x, norm_ffn1, w_gate1, w_up1, w_down1, norm_mix, w_in, w_conv, w_conv_out, attn_sinks,
           w_attn_out, w_out, norm_ffn2, w_gate2, w_up2, w_down2, norm_final):
    batch, seq, d = x.shape
    depth = norm_ffn1.shape[0]
    assert d == D_MODEL and seq % ROW_TILE == 0 and ROW_TILE % WINDOW == 0 and CONV_TILE == GROUP_WIDTH
    tiles_per_seq = seq // ROW_TILE
    bf = lambda w: w.astype(_bf16)
    gain = lambda g: g.reshape(1, D_MODEL).astype(_f32)
    slopes = 2.0 ** (-8.0 * jnp.arange(1, N_Q_HEADS + 1, dtype=_f32) / N_Q_HEADS)
    k_off = 4 * D_MODEL
    v_off = k_off + N_KV_HEADS * HEAD_DIM

    h = x.reshape(batch * seq, D_MODEL)
    gfin = gain(norm_final)
    for l in range(depth):
        last = l == depth - 1
        w_in_l = _col_blocks(w_in[l], CONV_TILE)
        wk = w_in[l][:, k_off:v_off].reshape(D_MODEL, N_KV_HEADS, HEAD_DIM)
        wv = w_in[l][:, v_off:v_off + N_KV_HEADS * HEAD_DIM].reshape(D_MODEL, N_KV_HEADS, HEAD_DIM)
        wkv = bf(jnp.concatenate([wk, wv], axis=-1).reshape(D_MODEL, N_KV_HEADS * LANES))

        h = _ffn(h, gain(norm_ffn1[l]), _col_blocks(w_gate1[l], FF_TILE), _col_blocks(w_up1[l], FF_TILE),
                 bf(w_down1[l]), gfin, False)
        g_mix = gain(norm_mix[l])
        mconv, u_mix = _conv_branch(h, g_mix, w_in_l, w_conv[l].astype(_f32), bf(w_conv_out[l]), tiles_per_seq)
        merged = _attn_branch(u_mix, mconv, w_in_l, wkv, bf(w_attn_out[l]), slopes,
                              attn_sinks[l].astype(_f32), tiles_per_seq)
        h = _oproj(h, merged, bf(w_out[l]))
        h = _ffn(h, gain(norm_ffn2[l]), _col_blocks(w_gate2[l], FF_TILE), _col_blocks(w_up2[l], FF_TILE),
                 bf(w_down2[l]), gfin, last)
    return h.reshape(batch, seq, D_MODEL)
```

```python
import functools

import jax
import jax.numpy as jnp
from jax import lax
from jax.experimental import pallas as pl
from jax.experimental.pallas import tpu as pltpu

D_MODEL = 2048
D_FF = 5632
CONV_K = 3
N_Q_HEADS = 32
N_KV_HEADS = 4
HEAD_DIM = 64
Q_GROUP = N_Q_HEADS // N_KV_HEADS
GROUP_WIDTH = Q_GROUP * HEAD_DIM
WINDOW = 128
RMS_EPS = 1e-5
ATTN_SCALE = HEAD_DIM ** -0.5

LANES = 128
SUBLANES = 8
VMEM_LIMIT_BYTES = 60 * 1024 * 1024

ROW_TILE = 512
FFN_ROW_TILE = 1024
FF_TILE = 512
NORM_ROWS = 128
CONV_TILE = 512

_f32 = jnp.float32
_bf16 = jnp.bfloat16


def _rms_norm_f32(x, gain):
    return x * lax.rsqrt(jnp.mean(x * x, axis=-1, keepdims=True) + RMS_EPS) * gain


def _dot(a, b):
    return jnp.dot(a, b, preferred_element_type=_f32)


def _serpentine(i, step, n):
    return jnp.where(i % 2 == 0, step, n - 1 - step)


def _ffn_kernel(h_ref, gain_ref, wg_ref, wu_ref, wd_ref, gfin_ref, out_ref, u_ref, *, final_norm):
    step = pl.program_id(1)

    def row_chunks(body, unroll):
        def chunk(c, carry):
            body(pl.ds(pl.multiple_of(c * NORM_ROWS, NORM_ROWS), NORM_ROWS))
            return carry
        lax.fori_loop(0, FFN_ROW_TILE // NORM_ROWS, chunk, 0, unroll=unroll)

    @pl.when(step == 0)
    def _():
        def body(rows):
            out_ref[rows, :] = jnp.zeros((NORM_ROWS, D_MODEL), _f32)
            u_ref[rows, :] = _rms_norm_f32(h_ref[rows, :], gain_ref[...]).astype(_bf16)
        row_chunks(body, unroll=2)

    u = u_ref[...]
    halves = []
    for c0 in range(0, FF_TILE, FF_TILE // 2):
        g = _dot(u, wg_ref[:, c0:c0 + FF_TILE // 2])
        up = _dot(u, wu_ref[:, c0:c0 + FF_TILE // 2])
        halves.append((g * jax.nn.sigmoid(g) * up).astype(_bf16))
    out_ref[...] += _dot(jnp.concatenate(halves, axis=1), wd_ref[...])

    @pl.when(step == pl.num_programs(1) - 1)
    def _():
        def body(rows):
            h = h_ref[rows, :] + 0.5 * out_ref[rows, :]
            if final_norm:
                h = _rms_norm_f32(h, gfin_ref[...])
            out_ref[rows, :] = h
        row_chunks(body, unroll=1)


def _ffn(h, gain, wg, wu, wd, gfin, final_norm):
    n = h.shape[0]
    nblk = D_FF // FF_TILE
    row = lambda i, s: (i, 0)
    const = lambda i, s: (0, 0)
    return pl.pallas_call(
        functools.partial(_ffn_kernel, final_norm=final_norm),
        grid=(n // FFN_ROW_TILE, nblk),
        in_specs=[
            pl.BlockSpec((FFN_ROW_TILE, D_MODEL), row),
            pl.BlockSpec((1, D_MODEL), const),
            pl.BlockSpec((D_MODEL, FF_TILE), lambda i, s: (0, _serpentine(i, s, nblk))),
            pl.BlockSpec((D_MODEL, FF_TILE), lambda i, s: (0, _serpentine(i, s, nblk))),
            pl.BlockSpec((FF_TILE, D_MODEL), lambda i, s: (_serpentine(i, s, nblk), 0)),
            pl.BlockSpec((1, D_MODEL), const),
        ],
        out_specs=pl.BlockSpec((FFN_ROW_TILE, D_MODEL), row),
        out_shape=jax.ShapeDtypeStruct((n, D_MODEL), _f32),
        scratch_shapes=[pltpu.VMEM((FFN_ROW_TILE, D_MODEL), _bf16)],
        compiler_params=pltpu.CompilerParams(
            dimension_semantics=("arbitrary", "arbitrary"),
            vmem_limit_bytes=VMEM_LIMIT_BYTES),
        name="ffn_final" if final_norm else "ffn",
    )(h, gain, wg, wu, wd, gfin)


def _conv_kernel(h_ref, gain_ref, wb_ref, wc_ref, wx_ref, wgc_ref, taps_ref, wco_ref, out_ref, u_ref,
                 acc_ref, gate_ref, ucbuf_ref, carry_ref, *, tiles_per_seq):
    i = pl.program_id(0)
    step = pl.program_id(1)
    j = _serpentine(i, step, pl.num_programs(1))

    @pl.when(step == 0)
    def _():
        u_ref[...] = _rms_norm_f32(h_ref[...], gain_ref[...]).astype(_bf16)
        acc_ref[...] = jnp.zeros_like(acc_ref)

    @pl.when(i % tiles_per_seq == 0)
    def _():
        carry_ref[j] = jnp.zeros((SUBLANES, CONV_TILE), _f32)

    u = u_ref[...]
    uc = _dot(u, wc_ref[...]) * _dot(u, wx_ref[...])
    ucbuf_ref[0:SUBLANES, :] = carry_ref[j]
    ucbuf_ref[SUBLANES:, :] = uc
    carry_ref[j] = uc[ROW_TILE - SUBLANES:, :]
    b = _dot(u, wb_ref[...])
    col = pl.multiple_of(j * CONV_TILE, CONV_TILE)
    gate_ref[:, pl.ds(col, CONV_TILE)] = jax.nn.sigmoid(_dot(u, wgc_ref[...]))
    taps = taps_ref[...]
    conv = taps[CONV_K - 1:CONV_K, :] * uc
    for tap in range(CONV_K - 1):
        back = CONV_K - 1 - tap
        conv = conv + taps[tap:tap + 1, :] * ucbuf_ref[SUBLANES - back:SUBLANES - back + ROW_TILE, :]
    acc_ref[...] += _dot((b * conv).astype(_bf16), wco_ref[...])

    @pl.when(step == pl.num_programs(1) - 1)
    def _():
        out_ref[...] = (gate_ref[...] * acc_ref[...]).astype(_bf16)


def _conv_branch(h, gain, w_in, taps, wco, tiles_per_seq):
    n = h.shape[0]
    nblk = D_MODEL // CONV_TILE
    row = lambda i, s: (i, 0)
    const = lambda i, s: (0, 0)
    gc_off = (4 * D_MODEL + 2 * N_KV_HEADS * HEAD_DIM) // CONV_TILE
    wcol = lambda blk_off: (lambda i, s: (0, _serpentine(i, s, nblk) + blk_off))
    return pl.pallas_call(
        functools.partial(_conv_kernel, tiles_per_seq=tiles_per_seq),
        grid=(n // ROW_TILE, nblk),
        in_specs=[
            pl.BlockSpec((ROW_TILE, D_MODEL), row),
            pl.BlockSpec((1, D_MODEL), const),
            pl.BlockSpec((D_MODEL, CONV_TILE), wcol(0)),
            pl.BlockSpec((D_MODEL, CONV_TILE), wcol(nblk)),
            pl.BlockSpec((D_MODEL, CONV_TILE), wcol(2 * nblk)),
            pl.BlockSpec((D_MODEL, CONV_TILE), wcol(gc_off)),
            pl.BlockSpec((CONV_K, CONV_TILE), lambda i, s: (0, _serpentine(i, s, nblk))),
            pl.BlockSpec((CONV_TILE, D_MODEL), lambda i, s: (_serpentine(i, s, nblk), 0)),
        ],
        out_specs=[pl.BlockSpec((ROW_TILE, D_MODEL), row), pl.BlockSpec((ROW_TILE, D_MODEL), row)],
        out_shape=[jax.ShapeDtypeStruct((n, D_MODEL), _bf16), jax.ShapeDtypeStruct((n, D_MODEL), _bf16)],
        scratch_shapes=[
            pltpu.VMEM((ROW_TILE, D_MODEL), _f32),
            pltpu.VMEM((ROW_TILE, D_MODEL), _f32),
            pltpu.VMEM((ROW_TILE + SUBLANES, CONV_TILE), _f32),
            pltpu.VMEM((nblk, SUBLANES, CONV_TILE), _f32),
        ],
        compiler_params=pltpu.CompilerParams(
            dimension_semantics=("arbitrary", "arbitrary"),
            vmem_limit_bytes=VMEM_LIMIT_BYTES),
        name="conv_branch",
    )(h, gain, w_in, w_in, w_in, w_in, taps, wco)


def _attn_kernel(slopes_ref, sinks_ref, u_ref, mconv_ref, wq_ref, wkv_ref, wga_ref, wao_ref,
                 out_ref, acc_ref, gate_ref, q_ref, a_ref, klo_ref, khi_ref, vlo_ref, vhi_ref,
                 *, tiles_per_seq):
    i = pl.program_id(0)
    step = pl.program_id(1)
    j = _serpentine(i, step, N_KV_HEADS)
    blocks = ROW_TILE // WINDOW
    pairs = Q_GROUP // 2
    first_tile = i % tiles_per_seq == 0
    kv_refs = (klo_ref, khi_ref, vlo_ref, vhi_ref)

    @pl.when(jnp.logical_and(step == 0, first_tile))
    def _():
        for ref in kv_refs:
            ref[:, 0:WINDOW, :] = jnp.zeros((N_KV_HEADS, WINDOW, LANES), _bf16)

    @pl.when(jnp.logical_and(step == 0, jnp.logical_not(first_tile)))
    def _():
        for ref in kv_refs:
            ref[:, 0:WINDOW, :] = ref[:, ROW_TILE:ROW_TILE + WINDOW, :]

    @pl.when(step == 0)
    def _():
        acc_ref[...] = jnp.zeros_like(acc_ref)
        kv_all = _dot(u_ref[...], wkv_ref[...])
        low = lax.broadcasted_iota(jnp.int32, (ROW_TILE, LANES), 1) < HEAD_DIM
        for g in range(N_KV_HEADS):
            kv = kv_all[:, g * LANES:(g + 1) * LANES]
            vk = pltpu.roll(kv, HEAD_DIM, axis=1)
            klo_ref[g, WINDOW:, :] = jnp.where(low, kv, 0.0).astype(_bf16)
            khi_ref[g, WINDOW:, :] = jnp.where(low, 0.0, vk).astype(_bf16)
            vlo_ref[g, WINDOW:, :] = jnp.where(low, vk, 0.0).astype(_bf16)
            vhi_ref[g, WINDOW:, :] = jnp.where(low, 0.0, kv).astype(_bf16)

    u = u_ref[...]
    q_ref[...] = (_dot(u, wq_ref[...]) * ATTN_SCALE).astype(_bf16)
    col = pl.multiple_of(j * GROUP_WIDTH, GROUP_WIDTH)
    gate_ref[:, pl.ds(col, GROUP_WIDTH)] = jax.nn.sigmoid(_dot(u, wga_ref[...]))

    qi = lax.broadcasted_iota(jnp.int32, (WINDOW, WINDOW), 0)
    kc = lax.broadcasted_iota(jnp.int32, (WINDOW, WINDOW), 1)
    in_cur = kc <= qi
    dist = (qi - kc + jnp.where(in_cur, 0, WINDOW)).astype(_f32)
    low_lane = kc < HEAD_DIM
    neg_inf = jnp.float32(-jnp.inf)
    sum_r = lax.broadcasted_iota(jnp.int32, (4 * WINDOW, LANES), 0) < 2 * WINDOW
    sum_c = lax.broadcasted_iota(jnp.int32, (4 * WINDOW, LANES), 1) < HEAD_DIM
    sum_cols = jnp.where(sum_r == sum_c, 1.0, 0.0).astype(_bf16)

    for blk in range(blocks):
        r0 = blk * WINDOW
        q_rows = jnp.concatenate(
            [q_ref[r0:r0 + WINDOW, pair * LANES:(pair + 1) * LANES] for pair in range(pairs)], axis=0)
        keys = jnp.concatenate([klo_ref[j, r0:r0 + 2 * WINDOW, :], khi_ref[j, r0:r0 + 2 * WINDOW, :]], axis=0)
        s_all = lax.dot_general(q_rows, keys, (((1,), (1,)), ((), ())), preferred_element_type=_f32)
        prob_rows = []
        sink_terms = []
        for pair in range(pairs):
            prob_cols = []
            sink_pair = []
            for half in range(2):
                head = j * Q_GROUP + 2 * pair + half
                c0 = half * 2 * WINDOW
                s_prev = s_all[pair * WINDOW:(pair + 1) * WINDOW, c0:c0 + WINDOW]
                s_cur = s_all[pair * WINDOW:(pair + 1) * WINDOW, c0 + WINDOW:c0 + 2 * WINDOW]
                if blk == 0:
                    s_prev = jnp.where(first_tile, neg_inf, s_prev)
                s = jnp.where(in_cur, s_cur, s_prev) - slopes_ref[head] * dist
                sink = sinks_ref[head]
                m = jnp.maximum(jnp.max(s, axis=-1, keepdims=True), sink)
                p = jnp.exp(s - m).astype(_bf16)
                zero = jnp.zeros_like(p)
                prob_cols += [jnp.where(in_cur, zero, p), jnp.where(in_cur, p, zero)]
                sink_pair.append(jnp.exp(sink - m))
            prob_rows.append(jnp.concatenate(prob_cols, axis=1))
            sink_terms.append(jnp.where(low_lane, sink_pair[0], sink_pair[1]))
        probs = jnp.concatenate(prob_rows, axis=0)
        vals = jnp.concatenate([vlo_ref[j, r0:r0 + 2 * WINDOW, :], vhi_ref[j, r0:r0 + 2 * WINDOW, :]], axis=0)
        o = _dot(probs, jnp.concatenate([vals, sum_cols], axis=1))
        for pair in range(pairs):
            rows = slice(pair * WINDOW, (pair + 1) * WINDOW)
            denom = o[rows, LANES:] + sink_terms[pair]
            a_ref[r0:r0 + WINDOW, pair * LANES:(pair + 1) * LANES] = (o[rows, :LANES] / denom).astype(_bf16)

    acc_ref[...] += _dot(a_ref[...], wao_ref[...])

    @pl.when(step == pl.num_programs(1) - 1)
    def _():
        merged = mconv_ref[...].astype(_f32) + gate_ref[...] * acc_ref[...]
        out_ref[...] = merged.astype(_bf16)


def _attn_branch(u, mconv, w_in, wkv, wao, slopes, sinks, tiles_per_seq):
    n = u.shape[0]
    q_off = 3 * D_MODEL // GROUP_WIDTH
    ga_off = (D_MODEL * 5 + 2 * N_KV_HEADS * HEAD_DIM) // GROUP_WIDTH
    row = lambda i, s: (i, 0)
    const = lambda i, s: (0, 0)
    group = lambda i, s: _serpentine(i, s, N_KV_HEADS)
    smem = pl.BlockSpec(memory_space=pltpu.SMEM)
    kv_scratch = pltpu.VMEM((N_KV_HEADS, ROW_TILE + WINDOW, LANES), _bf16)
    return pl.pallas_call(
        functools.partial(_attn_kernel, tiles_per_seq=tiles_per_seq),
        grid=(n // ROW_TILE, N_KV_HEADS),
        in_specs=[
            smem, smem,
            pl.BlockSpec((ROW_TILE, D_MODEL), row),
            pl.BlockSpec((ROW_TILE, D_MODEL), row),
            pl.BlockSpec((D_MODEL, GROUP_WIDTH), lambda i, s: (0, group(i, s) + q_off)),
            pl.BlockSpec((D_MODEL, N_KV_HEADS * LANES), const, pipeline_mode=pl.Buffered(1)),
            pl.BlockSpec((D_MODEL, GROUP_WIDTH), lambda i, s: (0, group(i, s) + ga_off)),
            pl.BlockSpec((GROUP_WIDTH, D_MODEL), lambda i, s: (group(i, s), 0)),
        ],
        out_specs=pl.BlockSpec((ROW_TILE, D_MODEL), row),
        out_shape=jax.ShapeDtypeStruct((n, D_MODEL), _bf16),
        scratch_shapes=[
            pltpu.VMEM((ROW_TILE, D_MODEL), _f32),
            pltpu.VMEM((ROW_TILE, D_MODEL), _f32),
            pltpu.VMEM((ROW_TILE, GROUP_WIDTH), _bf16),
            pltpu.VMEM((ROW_TILE, GROUP_WIDTH), _bf16),
            kv_scratch, kv_scratch, kv_scratch, kv_scratch,
        ],
        compiler_params=pltpu.CompilerParams(
            dimension_semantics=("arbitrary", "arbitrary"),
            vmem_limit_bytes=VMEM_LIMIT_BYTES),
        name="attn_branch",
    )(slopes, sinks, u, mconv, w_in, wkv, w_in, wao)


def _oproj_kernel(h_ref, m_ref, w_ref, out_ref):
    out_ref[...] = h_ref[...] + _dot(m_ref[...], w_ref[...])


def _oproj(h, merged, w_out):
    n = h.shape[0]
    row = lambda i: (i, 0)
    return pl.pallas_call(
        _oproj_kernel,
        grid=(n // ROW_TILE,),
        in_specs=[
            pl.BlockSpec((ROW_TILE, D_MODEL), row),
            pl.BlockSpec((ROW_TILE, D_MODEL), row),
            pl.BlockSpec((D_MODEL, D_MODEL), lambda i: (0, 0), pipeline_mode=pl.Buffered(1)),
        ],
        out_specs=pl.BlockSpec((ROW_TILE, D_MODEL), row),
        out_shape=jax.ShapeDtypeStruct((n, D_MODEL), _f32),
        compiler_params=pltpu.CompilerParams(
            dimension_semantics=("arbitrary",),
            vmem_limit_bytes=VMEM_LIMIT_BYTES),
        name="oproj",
    )(h, merged, w_out)


def kernel(x, norm_ffn1, w_gate1, w_up1, w_down1, norm_mix, w_in, w_conv, w_conv_out, attn_sinks,
           w_attn_out, w_out, norm_ffn2, w_gate2, w_up2, w_down2, norm_final):
    batch, seq, d = x.shape
    depth = norm_ffn1.shape[0]
    assert d == D_MODEL and seq % ROW_TILE == 0 and ROW_TILE % WINDOW == 0
    tiles_per_seq = seq // ROW_TILE
    bf = lambda w: w.astype(_bf16)
    gain = lambda g: g.reshape(1, D_MODEL).astype(_f32)
    slopes = 2.0 ** (-8.0 * jnp.arange(1, N_Q_HEADS + 1, dtype=_f32) / N_Q_HEADS)
    k_off = 4 * D_MODEL
    v_off = k_off + N_KV_HEADS * HEAD_DIM

    h = x.reshape(batch * seq, D_MODEL)
    gfin = gain(norm_final)
    for l in range(depth):
        last = l == depth - 1
        w_in_l = bf(w_in[l])
        wk = w_in_l[:, k_off:v_off].reshape(D_MODEL, N_KV_HEADS, HEAD_DIM)
        wv = w_in_l[:, v_off:v_off + N_KV_HEADS * HEAD_DIM].reshape(D_MODEL, N_KV_HEADS, HEAD_DIM)
        wkv = jnp.concatenate([wk, wv], axis=-1).reshape(D_MODEL, N_KV_HEADS * LANES)

        h = _ffn(h, gain(norm_ffn1[l]), bf(w_gate1[l]), bf(w_up1[l]), bf(w_down1[l]), gfin, False)
        g_mix = gain(norm_mix[l])
        mconv, u_mix = _conv_branch(h, g_mix, w_in_l, w_conv[l].astype(_f32), bf(w_conv_out[l]), tiles_per_seq)
        merged = _attn_branch(u_mix, mconv, w_in_l, wkv, bf(w_attn_out[l]), slopes,
                              attn_sinks[l].astype(_f32), tiles_per_seq)
        h = _oproj(h, merged, bf(w_out[l]))
        h = _ffn(h, gain(norm_ffn2[l]), bf(w_gate2[l]), bf(w_up2[l]), bf(w_down2[l]), gfin, last)
    return h.reshape(batch, seq, D_MODEL)
```

```python
import functools

import jax
import jax.numpy as jnp
from jax import lax
from jax.experimental import pallas as pl
from jax.experimental.pallas import tpu as pltpu

D_MODEL = 2048
D_FF = 5632
CONV_K = 3
N_Q_HEADS = 32
N_KV_HEADS = 4
HEAD_DIM = 64
Q_GROUP = N_Q_HEADS // N_KV_HEADS
GROUP_WIDTH = Q_GROUP * HEAD_DIM
WINDOW = 128
RMS_EPS = 1e-5
ATTN_SCALE = HEAD_DIM ** -0.5

LANES = 128
SUBLANES = 8
VMEM_LIMIT_BYTES = 60 * 1024 * 1024

ROW_TILE = 512
FFN_ROW_TILE = 1024
FF_TILE = 512
NORM_ROWS = 128
CONV_TILE = 512
assert D_FF // FF_TILE >= 2 and D_MODEL // CONV_TILE >= 2 and N_KV_HEADS >= 2

_f32 = jnp.float32
_bf16 = jnp.bfloat16


def _rms_norm_f32(x, gain):
    return x * lax.rsqrt(jnp.mean(x * x, axis=-1, keepdims=True) + RMS_EPS) * gain


def _dot(a, b):
    return jnp.dot(a, b, preferred_element_type=_f32)


def _ffn_kernel(h_ref, gain_ref, wg_ref, wu_ref, wd_ref, gfin_ref, out_ref, u_ref, *, final_norm):
    j = pl.program_id(1)
    last = pl.num_programs(1) - 1

    def swiglu_down():
        u = u_ref[...]
        halves = []
        for c0 in range(0, FF_TILE, FF_TILE // 2):
            g = _dot(u, wg_ref[:, c0:c0 + FF_TILE // 2])
            up = _dot(u, wu_ref[:, c0:c0 + FF_TILE // 2])
            halves.append((g * jax.nn.sigmoid(g) * up).astype(_bf16))
        return _dot(jnp.concatenate(halves, axis=1), wd_ref[...])

    @pl.when(j == 0)
    def _():
        u_ref[...] = _rms_norm_f32(h_ref[...], gain_ref[...]).astype(_bf16)
        out_ref[...] = swiglu_down()

    @pl.when(jnp.logical_and(j > 0, j < last))
    def _():
        out_ref[...] += swiglu_down()

    @pl.when(j == last)
    def _():
        out_ref[...] = h_ref[...] + 0.5 * (out_ref[...] + swiglu_down())
        if final_norm:
            def chunk(c, carry):
                rows = pl.ds(pl.multiple_of(c * NORM_ROWS, NORM_ROWS), NORM_ROWS)
                out_ref[rows, :] = _rms_norm_f32(out_ref[rows, :], gfin_ref[...])
                return carry
            lax.fori_loop(0, FFN_ROW_TILE // NORM_ROWS, chunk, 0)


def _ffn(h, gain, wg, wu, wd, gfin, final_norm):
    n = h.shape[0]
    row = lambda i, j: (i, 0)
    const = lambda i, j: (0, 0)
    return pl.pallas_call(
        functools.partial(_ffn_kernel, final_norm=final_norm),
        grid=(n // FFN_ROW_TILE, D_FF // FF_TILE),
        in_specs=[
            pl.BlockSpec((FFN_ROW_TILE, D_MODEL), row),
            pl.BlockSpec((1, D_MODEL), const),
            pl.BlockSpec((D_MODEL, FF_TILE), lambda i, j: (0, j)),
            pl.BlockSpec((D_MODEL, FF_TILE), lambda i, j: (0, j)),
            pl.BlockSpec((FF_TILE, D_MODEL), lambda i, j: (j, 0)),
            pl.BlockSpec((1, D_MODEL), const),
        ],
        out_specs=pl.BlockSpec((FFN_ROW_TILE, D_MODEL), row),
        out_shape=jax.ShapeDtypeStruct((n, D_MODEL), _f32),
        scratch_shapes=[pltpu.VMEM((FFN_ROW_TILE, D_MODEL), _bf16)],
        compiler_params=pltpu.CompilerParams(
            dimension_semantics=("arbitrary", "arbitrary"),
            vmem_limit_bytes=VMEM_LIMIT_BYTES),
        name="ffn_final" if final_norm else "ffn",
    )(h, gain, wg, wu, wd, gfin)


def _conv_kernel(h_ref, gain_ref, wb_ref, wc_ref, wx_ref, wgc_ref, taps_ref, wco_ref, out_ref, u_ref,
                 acc_ref, gate_ref, ucbuf_ref, carry_ref, *, tiles_per_seq):
    i = pl.program_id(0)
    j = pl.program_id(1)

    @pl.when(i % tiles_per_seq == 0)
    def _():
        carry_ref[j] = jnp.zeros((SUBLANES, CONV_TILE), _f32)

    def conv_out_part():
        u = u_ref[...]
        uc = _dot(u, wc_ref[...]) * _dot(u, wx_ref[...])
        ucbuf_ref[0:SUBLANES, :] = carry_ref[j]
        ucbuf_ref[SUBLANES:, :] = uc
        carry_ref[j] = uc[ROW_TILE - SUBLANES:, :]
        b = _dot(u, wb_ref[...])
        col = pl.multiple_of(j * CONV_TILE, CONV_TILE)
        gate_ref[:, pl.ds(col, CONV_TILE)] = jax.nn.sigmoid(_dot(u, wgc_ref[...]))
        taps = taps_ref[...]
        conv = taps[CONV_K - 1:CONV_K, :] * uc
        for tap in range(CONV_K - 1):
            back = CONV_K - 1 - tap
            conv = conv + taps[tap:tap + 1, :] * ucbuf_ref[SUBLANES - back:SUBLANES - back + ROW_TILE, :]
        return _dot((b * conv).astype(_bf16), wco_ref[...])

    last = pl.num_programs(1) - 1

    @pl.when(j == 0)
    def _():
        u_ref[...] = _rms_norm_f32(h_ref[...], gain_ref[...]).astype(_bf16)
        acc_ref[...] = conv_out_part()

    @pl.when(jnp.logical_and(j > 0, j < last))
    def _():
        acc_ref[...] += conv_out_part()

    @pl.when(j == last)
    def _():
        acc = acc_ref[...] + conv_out_part()
        out_ref[...] = (gate_ref[...] * acc).astype(_bf16)


def _conv_branch(h, gain, w_in, taps, wco, tiles_per_seq):
    n = h.shape[0]
    nblk = D_MODEL // CONV_TILE
    row = lambda i, j: (i, 0)
    const = lambda i, j: (0, 0)
    gc_off = (4 * D_MODEL + 2 * N_KV_HEADS * HEAD_DIM) // CONV_TILE
    wcol = lambda blk_off: (lambda i, j: (0, j + blk_off))
    return pl.pallas_call(
        functools.partial(_conv_kernel, tiles_per_seq=tiles_per_seq),
        grid=(n // ROW_TILE, nblk),
        in_specs=[
            pl.BlockSpec((ROW_TILE, D_MODEL), row),
            pl.BlockSpec((1, D_MODEL), const),
            pl.BlockSpec((D_MODEL, CONV_TILE), wcol(0)),
            pl.BlockSpec((D_MODEL, CONV_TILE), wcol(nblk)),
            pl.BlockSpec((D_MODEL, CONV_TILE), wcol(2 * nblk)),
            pl.BlockSpec((D_MODEL, CONV_TILE), wcol(gc_off)),
            pl.BlockSpec((CONV_K, CONV_TILE), lambda i, j: (0, j)),
            pl.BlockSpec((CONV_TILE, D_MODEL), lambda i, j: (j, 0)),
        ],
        out_specs=[pl.BlockSpec((ROW_TILE, D_MODEL), row), pl.BlockSpec((ROW_TILE, D_MODEL), row)],
        out_shape=[jax.ShapeDtypeStruct((n, D_MODEL), _bf16), jax.ShapeDtypeStruct((n, D_MODEL), _bf16)],
        scratch_shapes=[
            pltpu.VMEM((ROW_TILE, D_MODEL), _f32),
            pltpu.VMEM((ROW_TILE, D_MODEL), _f32),
            pltpu.VMEM((ROW_TILE + SUBLANES, CONV_TILE), _f32),
            pltpu.VMEM((nblk, SUBLANES, CONV_TILE), _f32),
        ],
        compiler_params=pltpu.CompilerParams(
            dimension_semantics=("arbitrary", "arbitrary"),
            vmem_limit_bytes=VMEM_LIMIT_BYTES),
        name="conv_branch",
    )(h, gain, w_in, w_in, w_in, w_in, taps, wco)


def _attn_kernel(slopes_ref, sinks_ref, u_ref, mconv_ref, wq_ref, wkv_ref, wga_ref, wao_ref,
                 out_ref, acc_ref, gate_ref, q_ref, a_ref, klo_ref, khi_ref, vlo_ref, vhi_ref,
                 *, tiles_per_seq):
    i = pl.program_id(0)
    j = pl.program_id(1)
    blocks = ROW_TILE // WINDOW
    pairs = Q_GROUP // 2
    first_tile = i % tiles_per_seq == 0
    kv_refs = (klo_ref, khi_ref, vlo_ref, vhi_ref)

    @pl.when(jnp.logical_and(j == 0, first_tile))
    def _():
        for ref in kv_refs:
            ref[:, 0:WINDOW, :] = jnp.zeros((N_KV_HEADS, WINDOW, LANES), _bf16)

    @pl.when(jnp.logical_and(j == 0, jnp.logical_not(first_tile)))
    def _():
        for ref in kv_refs:
            ref[:, 0:WINDOW, :] = ref[:, ROW_TILE:ROW_TILE + WINDOW, :]

    def project_kv():
        kv_all = _dot(u_ref[...], wkv_ref[...])
        low = lax.broadcasted_iota(jnp.int32, (ROW_TILE, LANES), 1) < HEAD_DIM
        for g in range(N_KV_HEADS):
            kv = kv_all[:, g * LANES:(g + 1) * LANES]
            vk = pltpu.roll(kv, HEAD_DIM, axis=1)
            klo_ref[g, WINDOW:, :] = jnp.where(low, kv, 0.0).astype(_bf16)
            khi_ref[g, WINDOW:, :] = jnp.where(low, 0.0, vk).astype(_bf16)
            vlo_ref[g, WINDOW:, :] = jnp.where(low, vk, 0.0).astype(_bf16)
            vhi_ref[g, WINDOW:, :] = jnp.where(low, 0.0, kv).astype(_bf16)

    def attention_out():
        u = u_ref[...]
        q_ref[...] = (_dot(u, wq_ref[...]) * ATTN_SCALE).astype(_bf16)
        col = pl.multiple_of(j * GROUP_WIDTH, GROUP_WIDTH)
        gate_ref[:, pl.ds(col, GROUP_WIDTH)] = jax.nn.sigmoid(_dot(u, wga_ref[...]))

        qi = lax.broadcasted_iota(jnp.int32, (WINDOW, WINDOW), 0)
        kc = lax.broadcasted_iota(jnp.int32, (WINDOW, WINDOW), 1)
        in_cur = kc <= qi
        dist = (qi - kc + jnp.where(in_cur, 0, WINDOW)).astype(_f32)
        low_lane = kc < HEAD_DIM
        neg_inf = jnp.float32(-jnp.inf)
        sum_r = lax.broadcasted_iota(jnp.int32, (4 * WINDOW, LANES), 0) < 2 * WINDOW
        sum_c = lax.broadcasted_iota(jnp.int32, (4 * WINDOW, LANES), 1) < HEAD_DIM
        sum_cols = jnp.where(sum_r == sum_c, 1.0, 0.0).astype(_bf16)

        for blk in range(blocks):
            r0 = blk * WINDOW
            q_rows = jnp.concatenate(
                [q_ref[r0:r0 + WINDOW, pair * LANES:(pair + 1) * LANES] for pair in range(pairs)], axis=0)
            keys = jnp.concatenate(
                [klo_ref[j, r0:r0 + 2 * WINDOW, :], khi_ref[j, r0:r0 + 2 * WINDOW, :]], axis=0)
            s_all = lax.dot_general(q_rows, keys, (((1,), (1,)), ((), ())), preferred_element_type=_f32)
            prob_rows = []
            sink_terms = []
            for pair in range(pairs):
                prob_cols = []
                sink_pair = []
                for half in range(2):
                    head = j * Q_GROUP + 2 * pair + half
                    c0 = half * 2 * WINDOW
                    s_prev = s_all[pair * WINDOW:(pair + 1) * WINDOW, c0:c0 + WINDOW]
                    s_cur = s_all[pair * WINDOW:(pair + 1) * WINDOW, c0 + WINDOW:c0 + 2 * WINDOW]
                    if blk == 0:
                        s_prev = jnp.where(first_tile, neg_inf, s_prev)
                    s = jnp.where(in_cur, s_cur, s_prev) - slopes_ref[head] * dist
                    sink = sinks_ref[head]
                    m = jnp.maximum(jnp.max(s, axis=-1, keepdims=True), sink)
                    p = jnp.exp(s - m).astype(_bf16)
                    zero = jnp.zeros_like(p)
                    prob_cols += [jnp.where(in_cur, zero, p), jnp.where(in_cur, p, zero)]
                    sink_pair.append(jnp.exp(sink - m))
                prob_rows.append(jnp.concatenate(prob_cols, axis=1))
                sink_terms.append(jnp.where(low_lane, sink_pair[0], sink_pair[1]))
            probs = jnp.concatenate(prob_rows, axis=0)
            vals = jnp.concatenate(
                [vlo_ref[j, r0:r0 + 2 * WINDOW, :], vhi_ref[j, r0:r0 + 2 * WINDOW, :]], axis=0)
            o = _dot(probs, jnp.concatenate([vals, sum_cols], axis=1))
            for pair in range(pairs):
                rows = slice(pair * WINDOW, (pair + 1) * WINDOW)
                denom = o[rows, LANES:] + sink_terms[pair]
                a_ref[r0:r0 + WINDOW, pair * LANES:(pair + 1) * LANES] = (
                    o[rows, :LANES] / denom).astype(_bf16)
        return _dot(a_ref[...], wao_ref[...])

    @pl.when(j == 0)
    def _():
        project_kv()
        acc_ref[...] = attention_out()

    last = pl.num_programs(1) - 1

    @pl.when(jnp.logical_and(j > 0, j < last))
    def _():
        acc_ref[...] += attention_out()

    @pl.when(j == last)
    def _():
        acc = acc_ref[...] + attention_out()
        out_ref[...] = (mconv_ref[...].astype(_f32) + gate_ref[...] * acc).astype(_bf16)


def _attn_branch(u, mconv, w_in, wkv, wao, slopes, sinks, tiles_per_seq):
    n = u.shape[0]
    q_off = 3 * D_MODEL // GROUP_WIDTH
    ga_off = (D_MODEL * 5 + 2 * N_KV_HEADS * HEAD_DIM) // GROUP_WIDTH
    row = lambda i, j: (i, 0)
    const = lambda i, j: (0, 0)
    smem = pl.BlockSpec(memory_space=pltpu.SMEM)
    kv_scratch = pltpu.VMEM((N_KV_HEADS, ROW_TILE + WINDOW, LANES), _bf16)
    return pl.pallas_call(
        functools.partial(_attn_kernel, tiles_per_seq=tiles_per_seq),
        grid=(n // ROW_TILE, N_KV_HEADS),
        in_specs=[
            smem, smem,
            pl.BlockSpec((ROW_TILE, D_MODEL), row),
            pl.BlockSpec((ROW_TILE, D_MODEL), row),
            pl.BlockSpec((D_MODEL, GROUP_WIDTH), lambda i, j: (0, j + q_off)),
            pl.BlockSpec((D_MODEL, N_KV_HEADS * LANES), const, pipeline_mode=pl.Buffered(1)),
            pl.BlockSpec((D_MODEL, GROUP_WIDTH), lambda i, j: (0, j + ga_off)),
            pl.BlockSpec((GROUP_WIDTH, D_MODEL), lambda i, j: (j, 0)),
        ],
        out_specs=pl.BlockSpec((ROW_TILE, D_MODEL), row),
        out_shape=jax.ShapeDtypeStruct((n, D_MODEL), _bf16),
        scratch_shapes=[
            pltpu.VMEM((ROW_TILE, D_MODEL), _f32),
            pltpu.VMEM((ROW_TILE, D_MODEL), _f32),
            pltpu.VMEM((ROW_TILE, GROUP_WIDTH), _bf16),
            pltpu.VMEM((ROW_TILE, GROUP_WIDTH), _bf16),
            kv_scratch, kv_scratch, kv_scratch, kv_scratch,
        ],
        compiler_params=pltpu.CompilerParams(
            dimension_semantics=("arbitrary", "arbitrary"),
            vmem_limit_bytes=VMEM_LIMIT_BYTES),
        name="attn_branch",
    )(slopes, sinks, u, mconv, w_in, wkv, w_in, wao)


def _oproj_kernel(h_ref, m_ref, w_ref, out_ref):
    out_ref[...] = h_ref[...] + _dot(m_ref[...], w_ref[...])


def _oproj(h, merged, w_out):
    n = h.shape[0]
    row = lambda i: (i, 0)
    return pl.pallas_call(
        _oproj_kernel,
        grid=(n // ROW_TILE,),
        in_specs=[
            pl.BlockSpec((ROW_TILE, D_MODEL), row),
            pl.BlockSpec((ROW_TILE, D_MODEL), row),
            pl.BlockSpec((D_MODEL, D_MODEL), lambda i: (0, 0), pipeline_mode=pl.Buffered(1)),
        ],
        out_specs=pl.BlockSpec((ROW_TILE, D_MODEL), row),
        out_shape=jax.ShapeDtypeStruct((n, D_MODEL), _f32),
        compiler_params=pltpu.CompilerParams(
            dimension_semantics=("arbitrary",),
            vmem_limit_bytes=VMEM_LIMIT_BYTES),
        name="oproj",
    )(h, merged, w_out)


def kernel(x, norm_ffn1, w_gate1, w_up1, w_down1, norm_mix, w_in, w_conv, w_conv_out, attn_sinks,
           w_attn_out, w_out, norm_ffn2, w_gate2, w_up2, w_down2, norm_final):
    batch, seq, d = x.shape
    depth = norm_ffn1.shape[0]
    assert d == D_MODEL and seq % ROW_TILE == 0 and ROW_TILE % WINDOW == 0
    tiles_per_seq = seq // ROW_TILE
    bf = lambda w: w.astype(_bf16)
    gain = lambda g: g.reshape(1, D_MODEL).astype(_f32)
    slopes = 2.0 ** (-8.0 * jnp.arange(1, N_Q_HEADS + 1, dtype=_f32) / N_Q_HEADS)
    k_off = 4 * D_MODEL
    v_off = k_off + N_KV_HEADS * HEAD_DIM

    h = x.reshape(batch * seq, D_MODEL)
    gfin = gain(norm_final)
    for l in range(depth):
        last = l == depth - 1
        w_in_l = bf(w_in[l])
        wk = w_in_l[:, k_off:v_off].reshape(D_MODEL, N_KV_HEADS, HEAD_DIM)
        wv = w_in_l[:, v_off:v_off + N_KV_HEADS * HEAD_DIM].reshape(D_MODEL, N_KV_HEADS, HEAD_DIM)
        wkv = jnp.concatenate([wk, wv], axis=-1).reshape(D_MODEL, N_KV_HEADS * LANES)

        h = _ffn(h, gain(norm_ffn1[l]), bf(w_gate1[l]), bf(w_up1[l]), bf(w_down1[l]), gfin, False)
        g_mix = gain(norm_mix[l])
        mconv, u_mix = _conv_branch(h, g_mix, w_in_l, w_conv[l].astype(_f32), bf(w_conv_out[l]), tiles_per_seq)
        merged = _attn_branch(u_mix, mconv, w_in_l, wkv, bf(w_attn_out[l]), slopes,
                              attn_sinks[l].astype(_f32), tiles_per_seq)
        h = _oproj(h, merged, bf(w_out[l]))
        h = _ffn(h, gain(norm_ffn2[l]), bf(w_gate2[l]), bf(w_up2[l]), bf(w_down2[l]), gfin, last)
    return h.reshape(batch, seq, D_MODEL)
```

```python
import functools

import jax
import jax.numpy as jnp
from jax import lax
from jax.experimental import pallas as pl
from jax.experimental.pallas import tpu as pltpu

D_MODEL = 2048
D_FF = 5632
CONV_K = 3
N_Q_HEADS = 32
N_KV_HEADS = 4
HEAD_DIM = 64
Q_GROUP = N_Q_HEADS // N_KV_HEADS
GROUP_WIDTH = Q_GROUP * HEAD_DIM
WINDOW = 128
RMS_EPS = 1e-5
ATTN_SCALE = HEAD_DIM ** -0.5

LANES = 128
SUBLANES = 8
VMEM_LIMIT_BYTES = 60 * 1024 * 1024

ROW_TILE = 512
FFN_ROW_TILE = 1024
FF_TILE = 512
NORM_ROWS = 128
CONV_TILE = 512
assert D_FF // FF_TILE >= 2 and D_MODEL // CONV_TILE >= 2

_f32 = jnp.float32
_bf16 = jnp.bfloat16


def _rms_norm_f32(x, gain):
    return x * lax.rsqrt(jnp.mean(x * x, axis=-1, keepdims=True) + RMS_EPS) * gain


def _dot(a, b):
    return jnp.dot(a, b, preferred_element_type=_f32)


def _ffn_kernel(h_ref, gain_ref, wg_ref, wu_ref, wd_ref, gfin_ref, out_ref, u_ref, *, final_norm):
    j = pl.program_id(1)
    last = pl.num_programs(1) - 1

    def swiglu_down():
        u = u_ref[...]
        halves = []
        for c0 in range(0, FF_TILE, FF_TILE // 2):
            g = _dot(u, wg_ref[:, c0:c0 + FF_TILE // 2])
            up = _dot(u, wu_ref[:, c0:c0 + FF_TILE // 2])
            halves.append((g * jax.nn.sigmoid(g) * up).astype(_bf16))
        return _dot(jnp.concatenate(halves, axis=1), wd_ref[...])

    @pl.when(j == 0)
    def _():
        u_ref[...] = _rms_norm_f32(h_ref[...], gain_ref[...]).astype(_bf16)
        out_ref[...] = swiglu_down()

    @pl.when(jnp.logical_and(j > 0, j < last))
    def _():
        out_ref[...] += swiglu_down()

    @pl.when(j == last)
    def _():
        out_ref[...] = h_ref[...] + 0.5 * (out_ref[...] + swiglu_down())
        if final_norm:
            def chunk(c, carry):
                rows = pl.ds(pl.multiple_of(c * NORM_ROWS, NORM_ROWS), NORM_ROWS)
                out_ref[rows, :] = _rms_norm_f32(out_ref[rows, :], gfin_ref[...])
                return carry
            lax.fori_loop(0, FFN_ROW_TILE // NORM_ROWS, chunk, 0)


def _ffn(h, gain, wg, wu, wd, gfin, final_norm):
    n = h.shape[0]
    row = lambda i, j: (i, 0)
    const = lambda i, j: (0, 0)
    return pl.pallas_call(
        functools.partial(_ffn_kernel, final_norm=final_norm),
        grid=(n // FFN_ROW_TILE, D_FF // FF_TILE),
        in_specs=[
            pl.BlockSpec((FFN_ROW_TILE, D_MODEL), row),
            pl.BlockSpec((1, D_MODEL), const),
            pl.BlockSpec((D_MODEL, FF_TILE), lambda i, j: (0, j)),
            pl.BlockSpec((D_MODEL, FF_TILE), lambda i, j: (0, j)),
            pl.BlockSpec((FF_TILE, D_MODEL), lambda i, j: (j, 0)),
            pl.BlockSpec((1, D_MODEL), const),
        ],
        out_specs=pl.BlockSpec((FFN_ROW_TILE, D_MODEL), row),
        out_shape=jax.ShapeDtypeStruct((n, D_MODEL), _f32),
        scratch_shapes=[pltpu.VMEM((FFN_ROW_TILE, D_MODEL), _bf16)],
        compiler_params=pltpu.CompilerParams(
            dimension_semantics=("arbitrary", "arbitrary"),
            vmem_limit_bytes=VMEM_LIMIT_BYTES),
        name="ffn_final" if final_norm else "ffn",
    )(h, gain, wg, wu, wd, gfin)


def _conv_kernel(h_ref, gain_ref, wb_ref, wc_ref, wx_ref, wgc_ref, taps_ref, wco_ref, out_ref, u_ref,
                 acc_ref, gate_ref, ucbuf_ref, carry_ref, *, tiles_per_seq):
    i = pl.program_id(0)
    j = pl.program_id(1)

    @pl.when(i % tiles_per_seq == 0)
    def _():
        carry_ref[j] = jnp.zeros((SUBLANES, CONV_TILE), _f32)

    def conv_out_part():
        u = u_ref[...]
        uc = _dot(u, wc_ref[...]) * _dot(u, wx_ref[...])
        ucbuf_ref[0:SUBLANES, :] = carry_ref[j]
        ucbuf_ref[SUBLANES:, :] = uc
        carry_ref[j] = uc[ROW_TILE - SUBLANES:, :]
        b = _dot(u, wb_ref[...])
        col = pl.multiple_of(j * CONV_TILE, CONV_TILE)
        gate_ref[:, pl.ds(col, CONV_TILE)] = jax.nn.sigmoid(_dot(u, wgc_ref[...]))
        taps = taps_ref[...]
        conv = taps[CONV_K - 1:CONV_K, :] * uc
        for tap in range(CONV_K - 1):
            back = CONV_K - 1 - tap
            conv = conv + taps[tap:tap + 1, :] * ucbuf_ref[SUBLANES - back:SUBLANES - back + ROW_TILE, :]
        return _dot((b * conv).astype(_bf16), wco_ref[...])

    last = pl.num_programs(1) - 1

    @pl.when(j == 0)
    def _():
        u_ref[...] = _rms_norm_f32(h_ref[...], gain_ref[...]).astype(_bf16)
        acc_ref[...] = conv_out_part()

    @pl.when(jnp.logical_and(j > 0, j < last))
    def _():
        acc_ref[...] += conv_out_part()

    @pl.when(j == last)
    def _():
        acc = acc_ref[...] + conv_out_part()
        out_ref[...] = (gate_ref[...] * acc).astype(_bf16)


def _conv_branch(h, gain, w_in, taps, wco, tiles_per_seq):
    n = h.shape[0]
    nblk = D_MODEL // CONV_TILE
    row = lambda i, j: (i, 0)
    const = lambda i, j: (0, 0)
    gc_off = (4 * D_MODEL + 2 * N_KV_HEADS * HEAD_DIM) // CONV_TILE
    wcol = lambda blk_off: (lambda i, j: (0, j + blk_off))
    return pl.pallas_call(
        functools.partial(_conv_kernel, tiles_per_seq=tiles_per_seq),
        grid=(n // ROW_TILE, nblk),
        in_specs=[
            pl.BlockSpec((ROW_TILE, D_MODEL), row),
            pl.BlockSpec((1, D_MODEL), const),
            pl.BlockSpec((D_MODEL, CONV_TILE), wcol(0)),
            pl.BlockSpec((D_MODEL, CONV_TILE), wcol(nblk)),
            pl.BlockSpec((D_MODEL, CONV_TILE), wcol(2 * nblk)),
            pl.BlockSpec((D_MODEL, CONV_TILE), wcol(gc_off)),
            pl.BlockSpec((CONV_K, CONV_TILE), lambda i, j: (0, j)),
            pl.BlockSpec((CONV_TILE, D_MODEL), lambda i, j: (j, 0)),
        ],
        out_specs=[pl.BlockSpec((ROW_TILE, D_MODEL), row), pl.BlockSpec((ROW_TILE, D_MODEL), row)],
        out_shape=[jax.ShapeDtypeStruct((n, D_MODEL), _bf16), jax.ShapeDtypeStruct((n, D_MODEL), _bf16)],
        scratch_shapes=[
            pltpu.VMEM((ROW_TILE, D_MODEL), _f32),
            pltpu.VMEM((ROW_TILE, D_MODEL), _f32),
            pltpu.VMEM((ROW_TILE + SUBLANES, CONV_TILE), _f32),
            pltpu.VMEM((nblk, SUBLANES, CONV_TILE), _f32),
        ],
        compiler_params=pltpu.CompilerParams(
            dimension_semantics=("arbitrary", "arbitrary"),
            vmem_limit_bytes=VMEM_LIMIT_BYTES),
        name="conv_branch",
    )(h, gain, w_in, w_in, w_in, w_in, taps, wco)


def _attn_kernel(slopes_ref, sinks_ref, u_ref, mconv_ref, wq_ref, wkv_ref, wga_ref, wao_ref,
                 out_ref, gate_ref, q_ref, a_ref, klo_ref, khi_ref, vlo_ref, vhi_ref,
                 *, tiles_per_seq):
    i = pl.program_id(0)
    blocks = ROW_TILE // WINDOW
    pairs = Q_GROUP // 2
    first_tile = i % tiles_per_seq == 0
    kv_refs = (klo_ref, khi_ref, vlo_ref, vhi_ref)

    @pl.when(first_tile)
    def _():
        for ref in kv_refs:
            ref[:, 0:WINDOW, :] = jnp.zeros((N_KV_HEADS, WINDOW, LANES), _bf16)

    @pl.when(jnp.logical_not(first_tile))
    def _():
        for ref in kv_refs:
            ref[:, 0:WINDOW, :] = ref[:, ROW_TILE:ROW_TILE + WINDOW, :]

    def project_kv():
        kv_all = _dot(u_ref[...], wkv_ref[...])
        low = lax.broadcasted_iota(jnp.int32, (ROW_TILE, LANES), 1) < HEAD_DIM
        for g in range(N_KV_HEADS):
            kv = kv_all[:, g * LANES:(g + 1) * LANES]
            vk = pltpu.roll(kv, HEAD_DIM, axis=1)
            klo_ref[g, WINDOW:, :] = jnp.where(low, kv, 0.0).astype(_bf16)
            khi_ref[g, WINDOW:, :] = jnp.where(low, 0.0, vk).astype(_bf16)
            vlo_ref[g, WINDOW:, :] = jnp.where(low, vk, 0.0).astype(_bf16)
            vhi_ref[g, WINDOW:, :] = jnp.where(low, 0.0, kv).astype(_bf16)

    def attend(g):
        qi = lax.broadcasted_iota(jnp.int32, (WINDOW, WINDOW), 0)
        kc = lax.broadcasted_iota(jnp.int32, (WINDOW, WINDOW), 1)
        in_cur = kc <= qi
        dist = (qi - kc + jnp.where(in_cur, 0, WINDOW)).astype(_f32)
        low_lane = kc < HEAD_DIM
        neg_inf = jnp.float32(-jnp.inf)
        sum_r = lax.broadcasted_iota(jnp.int32, (4 * WINDOW, LANES), 0) < 2 * WINDOW
        sum_c = lax.broadcasted_iota(jnp.int32, (4 * WINDOW, LANES), 1) < HEAD_DIM
        sum_cols = jnp.where(sum_r == sum_c, 1.0, 0.0).astype(_bf16)

        for blk in range(blocks):
            r0 = blk * WINDOW
            cols = [slice(g * GROUP_WIDTH + pair * LANES, g * GROUP_WIDTH + (pair + 1) * LANES)
                    for pair in range(pairs)]
            q_rows = jnp.concatenate([q_ref[r0:r0 + WINDOW, c] for c in cols], axis=0)
            keys = jnp.concatenate(
                [klo_ref[g, r0:r0 + 2 * WINDOW, :], khi_ref[g, r0:r0 + 2 * WINDOW, :]], axis=0)
            s_all = lax.dot_general(q_rows, keys, (((1,), (1,)), ((), ())), preferred_element_type=_f32)
            prob_rows = []
            sink_terms = []
            for pair in range(pairs):
                prob_cols = []
                sink_pair = []
                for half in range(2):
                    head = g * Q_GROUP + 2 * pair + half
                    c0 = half * 2 * WINDOW
                    s_prev = s_all[pair * WINDOW:(pair + 1) * WINDOW, c0:c0 + WINDOW]
                    s_cur = s_all[pair * WINDOW:(pair + 1) * WINDOW, c0 + WINDOW:c0 + 2 * WINDOW]
                    if blk == 0:
                        s_prev = jnp.where(first_tile, neg_inf, s_prev)
                    s = jnp.where(in_cur, s_cur, s_prev) - slopes_ref[head] * dist
                    sink = sinks_ref[head]
                    m = jnp.maximum(jnp.max(s, axis=-1, keepdims=True), sink)
                    p = jnp.exp(s - m).astype(_bf16)
                    zero = jnp.zeros_like(p)
                    prob_cols += [jnp.where(in_cur, zero, p), jnp.where(in_cur, p, zero)]
                    sink_pair.append(jnp.exp(sink - m))
                prob_rows.append(jnp.concatenate(prob_cols, axis=1))
                sink_terms.append(jnp.where(low_lane, sink_pair[0], sink_pair[1]))
            probs = jnp.concatenate(prob_rows, axis=0)
            vals = jnp.concatenate(
                [vlo_ref[g, r0:r0 + 2 * WINDOW, :], vhi_ref[g, r0:r0 + 2 * WINDOW, :]], axis=0)
            o = _dot(probs, jnp.concatenate([vals, sum_cols], axis=1))
            for pair in range(pairs):
                rows = slice(pair * WINDOW, (pair + 1) * WINDOW)
                denom = o[rows, LANES:] + sink_terms[pair]
                a_ref[r0:r0 + WINDOW, cols[pair]] = (o[rows, :LANES] / denom).astype(_bf16)

    u = u_ref[...]
    project_kv()
    for g in range(N_KV_HEADS):
        cols = slice(g * GROUP_WIDTH, (g + 1) * GROUP_WIDTH)
        q_ref[:, cols] = (_dot(u, wq_ref[:, cols]) * ATTN_SCALE).astype(_bf16)
        gate_ref[:, cols] = jax.nn.sigmoid(_dot(u, wga_ref[:, cols]))
        attend(g)
    y_attn = _dot(a_ref[...], wao_ref[...])
    out_ref[...] = (mconv_ref[...].astype(_f32) + gate_ref[...] * y_attn).astype(_bf16)


def _attn_branch(u, mconv, wq, wkv, wga, wao, slopes, sinks, tiles_per_seq):
    n = u.shape[0]
    row = lambda i: (i, 0)
    resident = lambda shape: pl.BlockSpec(shape, lambda i: (0, 0), pipeline_mode=pl.Buffered(1))
    smem = pl.BlockSpec(memory_space=pltpu.SMEM)
    kv_scratch = pltpu.VMEM((N_KV_HEADS, ROW_TILE + WINDOW, LANES), _bf16)
    return pl.pallas_call(
        functools.partial(_attn_kernel, tiles_per_seq=tiles_per_seq),
        grid=(n // ROW_TILE,),
        in_specs=[
            smem, smem,
            pl.BlockSpec((ROW_TILE, D_MODEL), row),
            pl.BlockSpec((ROW_TILE, D_MODEL), row),
            resident((D_MODEL, N_Q_HEADS * HEAD_DIM)),
            resident((D_MODEL, N_KV_HEADS * LANES)),
            resident((D_MODEL, D_MODEL)),
            resident((N_Q_HEADS * HEAD_DIM, D_MODEL)),
        ],
        out_specs=pl.BlockSpec((ROW_TILE, D_MODEL), row),
        out_shape=jax.ShapeDtypeStruct((n, D_MODEL), _bf16),
        scratch_shapes=[
            pltpu.VMEM((ROW_TILE, D_MODEL), _f32),
            pltpu.VMEM((ROW_TILE, N_Q_HEADS * HEAD_DIM), _bf16),
            pltpu.VMEM((ROW_TILE, N_Q_HEADS * HEAD_DIM), _bf16),
            kv_scratch, kv_scratch, kv_scratch, kv_scratch,
        ],
        compiler_params=pltpu.CompilerParams(
            dimension_semantics=("arbitrary",),
            vmem_limit_bytes=VMEM_LIMIT_BYTES),
        name="attn_branch",
    )(slopes, sinks, u, mconv, wq, wkv, wga, wao)


def _oproj_kernel(h_ref, m_ref, w_ref, out_ref):
    out_ref[...] = h_ref[...] + _dot(m_ref[...], w_ref[...])


def _oproj(h, merged, w_out):
    n = h.shape[0]
    row = lambda i: (i, 0)
    return pl.pallas_call(
        _oproj_kernel,
        grid=(n // ROW_TILE,),
        in_specs=[
            pl.BlockSpec((ROW_TILE, D_MODEL), row),
            pl.BlockSpec((ROW_TILE, D_MODEL), row),
            pl.BlockSpec((D_MODEL, D_MODEL), lambda i: (0, 0), pipeline_mode=pl.Buffered(1)),
        ],
        out_specs=pl.BlockSpec((ROW_TILE, D_MODEL), row),
        out_shape=jax.ShapeDtypeStruct((n, D_MODEL), _f32),
        compiler_params=pltpu.CompilerParams(
            dimension_semantics=("arbitrary",),
            vmem_limit_bytes=VMEM_LIMIT_BYTES),
        name="oproj",
    )(h, merged, w_out)


def kernel(x, norm_ffn1, w_gate1, w_up1, w_down1, norm_mix, w_in, w_conv, w_conv_out, attn_sinks,
           w_attn_out, w_out, norm_ffn2, w_gate2, w_up2, w_down2, norm_final):
    batch, seq, d = x.shape
    depth = norm_ffn1.shape[0]
    assert d == D_MODEL and seq % ROW_TILE == 0 and ROW_TILE % WINDOW == 0
    tiles_per_seq = seq // ROW_TILE
    bf = lambda w: w.astype(_bf16)
    gain = lambda g: g.reshape(1, D_MODEL).astype(_f32)
    slopes = 2.0 ** (-8.0 * jnp.arange(1, N_Q_HEADS + 1, dtype=_f32) / N_Q_HEADS)
    k_off = 4 * D_MODEL
    v_off = k_off + N_KV_HEADS * HEAD_DIM

    h = x.reshape(batch * seq, D_MODEL)
    gfin = gain(norm_final)
    for l in range(depth):
        last = l == depth - 1
        w_in_l = bf(w_in[l])
        wk = w_in_l[:, k_off:v_off].reshape(D_MODEL, N_KV_HEADS, HEAD_DIM)
        wv = w_in_l[:, v_off:v_off + N_KV_HEADS * HEAD_DIM].reshape(D_MODEL, N_KV_HEADS, HEAD_DIM)
        wkv = jnp.concatenate([wk, wv], axis=-1).reshape(D_MODEL, N_KV_HEADS * LANES)
        wq = w_in_l[:, 3 * D_MODEL:k_off]
        wga = w_in_l[:, -D_MODEL:]

        h = _ffn(h, gain(norm_ffn1[l]), bf(w_gate1[l]), bf(w_up1[l]), bf(w_down1[l]), gfin, False)
        g_mix = gain(norm_mix[l])
        mconv, u_mix = _conv_branch(h, g_mix, w_in_l, w_conv[l].astype(_f32), bf(w_conv_out[l]), tiles_per_seq)
        merged = _attn_branch(u_mix, mconv, wq, wkv, wga, bf(w_attn_out[l]), slopes,
                              attn_sinks[l].astype(_f32), tiles_per_seq)
        h = _oproj(h, merged, bf(w_out[l]))
        h = _ffn(h, gain(norm_ffn2[l]), bf(w_gate2[l]), bf(w_up2[l]), bf(w_down2[l]), gfin, last)
    return h.reshape(batch, seq, D_MODEL)
```

```python
import functools

import jax
import jax.numpy as jnp
from jax import lax
from jax.experimental import pallas as pl
from jax.experimental.pallas import tpu as pltpu

D_MODEL = 2048
D_FF = 5632
CONV_K = 3
N_Q_HEADS = 32
N_KV_HEADS = 4
HEAD_DIM = 64
Q_GROUP = N_Q_HEADS // N_KV_HEADS
GROUP_WIDTH = Q_GROUP * HEAD_DIM
WINDOW = 128
RMS_EPS = 1e-5
ATTN_SCALE = HEAD_DIM ** -0.5

LANES = 128
SUBLANES = 8
VMEM_LIMIT_BYTES = 60 * 1024 * 1024

ROW_TILE = 512
FFN_ROW_TILE = 1024
FF_TILE = 512
NORM_ROWS = 128
CONV_TILE = 512
assert D_FF // FF_TILE >= 2 and D_MODEL // CONV_TILE >= 2

_f32 = jnp.float32
_bf16 = jnp.bfloat16


def _rms_norm_f32(x, gain):
    return x * lax.rsqrt(jnp.mean(x * x, axis=-1, keepdims=True) + RMS_EPS) * gain


def _dot(a, b):
    return jnp.dot(a, b, preferred_element_type=_f32)


def _ffn_kernel(h_ref, gain_ref, wg_ref, wu_ref, wd_ref, gfin_ref, out_ref, u_ref, *, final_norm):
    j = pl.program_id(1)
    last = pl.num_programs(1) - 1

    def swiglu_down():
        u = u_ref[...]
        halves = []
        for c0 in range(0, FF_TILE, FF_TILE // 2):
            g = _dot(u, wg_ref[:, c0:c0 + FF_TILE // 2])
            up = _dot(u, wu_ref[:, c0:c0 + FF_TILE // 2])
            halves.append((g * jax.nn.sigmoid(g) * up).astype(_bf16))
        return _dot(jnp.concatenate(halves, axis=1), wd_ref[...])

    @pl.when(j == 0)
    def _():
        u_ref[...] = _rms_norm_f32(h_ref[...], gain_ref[...]).astype(_bf16)
        out_ref[...] = swiglu_down()

    @pl.when(jnp.logical_and(j > 0, j < last))
    def _():
        out_ref[...] += swiglu_down()

    @pl.when(j == last)
    def _():
        out_ref[...] = h_ref[...] + 0.5 * (out_ref[...] + swiglu_down())
        if final_norm:
            def chunk(c, carry):
                rows = pl.ds(pl.multiple_of(c * NORM_ROWS, NORM_ROWS), NORM_ROWS)
                out_ref[rows, :] = _rms_norm_f32(out_ref[rows, :], gfin_ref[...])
                return carry
            lax.fori_loop(0, FFN_ROW_TILE // NORM_ROWS, chunk, 0)


def _ffn(h, gain, wg, wu, wd, gfin, final_norm):
    n = h.shape[0]
    row = lambda i, j: (i, 0)
    const = lambda i, j: (0, 0)
    return pl.pallas_call(
        functools.partial(_ffn_kernel, final_norm=final_norm),
        grid=(n // FFN_ROW_TILE, D_FF // FF_TILE),
        in_specs=[
            pl.BlockSpec((FFN_ROW_TILE, D_MODEL), row),
            pl.BlockSpec((1, D_MODEL), const),
            pl.BlockSpec((D_MODEL, FF_TILE), lambda i, j: (0, j)),
            pl.BlockSpec((D_MODEL, FF_TILE), lambda i, j: (0, j)),
            pl.BlockSpec((FF_TILE, D_MODEL), lambda i, j: (j, 0)),
            pl.BlockSpec((1, D_MODEL), const),
        ],
        out_specs=pl.BlockSpec((FFN_ROW_TILE, D_MODEL), row),
        out_shape=jax.ShapeDtypeStruct((n, D_MODEL), _f32),
        scratch_shapes=[pltpu.VMEM((FFN_ROW_TILE, D_MODEL), _bf16)],
        compiler_params=pltpu.CompilerParams(
            dimension_semantics=("arbitrary", "arbitrary"),
            vmem_limit_bytes=VMEM_LIMIT_BYTES),
        name="ffn_final" if final_norm else "ffn",
    )(h, gain, wg, wu, wd, gfin)


def _conv_kernel(h_ref, gain_ref, wb_ref, wc_ref, wx_ref, wgc_ref, taps_ref, wco_ref, out_ref, u_ref,
                 acc_ref, gate_ref, ucbuf_ref, carry_ref, *, tiles_per_seq):
    i = pl.program_id(0)
    j = pl.program_id(1)

    @pl.when(i % tiles_per_seq == 0)
    def _():
        carry_ref[j] = jnp.zeros((SUBLANES, CONV_TILE), _f32)

    def conv_out_part():
        u = u_ref[...]
        uc = _dot(u, wc_ref[...]) * _dot(u, wx_ref[...])
        ucbuf_ref[0:SUBLANES, :] = carry_ref[j]
        ucbuf_ref[SUBLANES:, :] = uc
        carry_ref[j] = uc[ROW_TILE - SUBLANES:, :]
        b = _dot(u, wb_ref[...])
        col = pl.multiple_of(j * CONV_TILE, CONV_TILE)
        gate_ref[:, pl.ds(col, CONV_TILE)] = jax.nn.sigmoid(_dot(u, wgc_ref[...]))
        taps = taps_ref[...]
        conv = taps[CONV_K - 1:CONV_K, :] * uc
        for tap in range(CONV_K - 1):
            back = CONV_K - 1 - tap
            conv = conv + taps[tap:tap + 1, :] * ucbuf_ref[SUBLANES - back:SUBLANES - back + ROW_TILE, :]
        return _dot((b * conv).astype(_bf16), wco_ref[...])

    last = pl.num_programs(1) - 1

    @pl.when(j == 0)
    def _():
        u_ref[...] = _rms_norm_f32(h_ref[...], gain_ref[...]).astype(_bf16)
        acc_ref[...] = conv_out_part()

    @pl.when(jnp.logical_and(j > 0, j < last))
    def _():
        acc_ref[...] += conv_out_part()

    @pl.when(j == last)
    def _():
        acc = acc_ref[...] + conv_out_part()
        out_ref[...] = (gate_ref[...] * acc).astype(_bf16)


def _conv_branch(h, gain, w_in, taps, wco, tiles_per_seq):
    n = h.shape[0]
    nblk = D_MODEL // CONV_TILE
    row = lambda i, j: (i, 0)
    const = lambda i, j: (0, 0)
    gc_off = (4 * D_MODEL + 2 * N_KV_HEADS * HEAD_DIM) // CONV_TILE
    wcol = lambda blk_off: (lambda i, j: (0, j + blk_off))
    return pl.pallas_call(
        functools.partial(_conv_kernel, tiles_per_seq=tiles_per_seq),
        grid=(n // ROW_TILE, nblk),
        in_specs=[
            pl.BlockSpec((ROW_TILE, D_MODEL), row),
            pl.BlockSpec((1, D_MODEL), const),
            pl.BlockSpec((D_MODEL, CONV_TILE), wcol(0)),
            pl.BlockSpec((D_MODEL, CONV_TILE), wcol(nblk)),
            pl.BlockSpec((D_MODEL, CONV_TILE), wcol(2 * nblk)),
            pl.BlockSpec((D_MODEL, CONV_TILE), wcol(gc_off)),
            pl.BlockSpec((CONV_K, CONV_TILE), lambda i, j: (0, j)),
            pl.BlockSpec((CONV_TILE, D_MODEL), lambda i, j: (j, 0)),
        ],
        out_specs=[pl.BlockSpec((ROW_TILE, D_MODEL), row), pl.BlockSpec((ROW_TILE, D_MODEL), row)],
        out_shape=[jax.ShapeDtypeStruct((n, D_MODEL), _bf16), jax.ShapeDtypeStruct((n, D_MODEL), _bf16)],
        scratch_shapes=[
            pltpu.VMEM((ROW_TILE, D_MODEL), _f32),
            pltpu.VMEM((ROW_TILE, D_MODEL), _f32),
            pltpu.VMEM((ROW_TILE + SUBLANES, CONV_TILE), _f32),
            pltpu.VMEM((nblk, SUBLANES, CONV_TILE), _f32),
        ],
        compiler_params=pltpu.CompilerParams(
            dimension_semantics=("arbitrary", "arbitrary"),
            vmem_limit_bytes=VMEM_LIMIT_BYTES),
        name="conv_branch",
    )(h, gain, w_in, w_in, w_in, w_in, taps, wco)


def _attn_kernel(slopes_ref, sinks_ref, u_ref, mconv_ref, wkv_ref, wao_ref, *refs, tiles_per_seq):
    wq_refs = refs[:N_KV_HEADS]
    wga_refs = refs[N_KV_HEADS:2 * N_KV_HEADS]
    out_ref, gate_ref, q_ref, a_ref, klo_ref, khi_ref, vlo_ref, vhi_ref = refs[2 * N_KV_HEADS:]
    i = pl.program_id(0)
    blocks = ROW_TILE // WINDOW
    pairs = Q_GROUP // 2
    first_tile = i % tiles_per_seq == 0
    kv_refs = (klo_ref, khi_ref, vlo_ref, vhi_ref)

    @pl.when(first_tile)
    def _():
        for ref in kv_refs:
            ref[:, 0:WINDOW, :] = jnp.zeros((N_KV_HEADS, WINDOW, LANES), _bf16)

    @pl.when(jnp.logical_not(first_tile))
    def _():
        for ref in kv_refs:
            ref[:, 0:WINDOW, :] = ref[:, ROW_TILE:ROW_TILE + WINDOW, :]

    def project_kv():
        kv_all = _dot(u_ref[...], wkv_ref[...])
        low = lax.broadcasted_iota(jnp.int32, (ROW_TILE, LANES), 1) < HEAD_DIM
        for g in range(N_KV_HEADS):
            kv = kv_all[:, g * LANES:(g + 1) * LANES]
            vk = pltpu.roll(kv, HEAD_DIM, axis=1)
            klo_ref[g, WINDOW:, :] = jnp.where(low, kv, 0.0).astype(_bf16)
            khi_ref[g, WINDOW:, :] = jnp.where(low, 0.0, vk).astype(_bf16)
            vlo_ref[g, WINDOW:, :] = jnp.where(low, vk, 0.0).astype(_bf16)
            vhi_ref[g, WINDOW:, :] = jnp.where(low, 0.0, kv).astype(_bf16)

    def attend(g):
        qi = lax.broadcasted_iota(jnp.int32, (WINDOW, WINDOW), 0)
        kc = lax.broadcasted_iota(jnp.int32, (WINDOW, WINDOW), 1)
        in_cur = kc <= qi
        dist = (qi - kc + jnp.where(in_cur, 0, WINDOW)).astype(_f32)
        low_lane = kc < HEAD_DIM
        neg_inf = jnp.float32(-jnp.inf)
        sum_r = lax.broadcasted_iota(jnp.int32, (4 * WINDOW, LANES), 0) < 2 * WINDOW
        sum_c = lax.broadcasted_iota(jnp.int32, (4 * WINDOW, LANES), 1) < HEAD_DIM
        sum_cols = jnp.where(sum_r == sum_c, 1.0, 0.0).astype(_bf16)

        for blk in range(blocks):
            r0 = blk * WINDOW
            cols = [slice(g * GROUP_WIDTH + pair * LANES, g * GROUP_WIDTH + (pair + 1) * LANES)
                    for pair in range(pairs)]
            q_rows = jnp.concatenate([q_ref[r0:r0 + WINDOW, c] for c in cols], axis=0)
            keys = jnp.concatenate(
                [klo_ref[g, r0:r0 + 2 * WINDOW, :], khi_ref[g, r0:r0 + 2 * WINDOW, :]], axis=0)
            s_all = lax.dot_general(q_rows, keys, (((1,), (1,)), ((), ())), preferred_element_type=_f32)
            prob_rows = []
            sink_terms = []
            for pair in range(pairs):
                prob_cols = []
                sink_pair = []
                for half in range(2):
                    head = g * Q_GROUP + 2 * pair + half
                    c0 = half * 2 * WINDOW
                    s_prev = s_all[pair * WINDOW:(pair + 1) * WINDOW, c0:c0 + WINDOW]
                    s_cur = s_all[pair * WINDOW:(pair + 1) * WINDOW, c0 + WINDOW:c0 + 2 * WINDOW]
                    if blk == 0:
                        s_prev = jnp.where(first_tile, neg_inf, s_prev)
                    s = jnp.where(in_cur, s_cur, s_prev) - slopes_ref[head] * dist
                    sink = sinks_ref[head]
                    m = jnp.maximum(jnp.max(s, axis=-1, keepdims=True), sink)
                    p = jnp.exp(s - m).astype(_bf16)
                    zero = jnp.zeros_like(p)
                    prob_cols += [jnp.where(in_cur, zero, p), jnp.where(in_cur, p, zero)]
                    sink_pair.append(jnp.exp(sink - m))
                prob_rows.append(jnp.concatenate(prob_cols, axis=1))
                sink_terms.append(jnp.where(low_lane, sink_pair[0], sink_pair[1]))
            probs = jnp.concatenate(prob_rows, axis=0)
            vals = jnp.concatenate(
                [vlo_ref[g, r0:r0 + 2 * WINDOW, :], vhi_ref[g, r0:r0 + 2 * WINDOW, :]], axis=0)
            o = _dot(probs, jnp.concatenate([vals, sum_cols], axis=1))
            for pair in range(pairs):
                rows = slice(pair * WINDOW, (pair + 1) * WINDOW)
                denom = o[rows, LANES:] + sink_terms[pair]
                a_ref[r0:r0 + WINDOW, cols[pair]] = (o[rows, :LANES] / denom).astype(_bf16)

    u = u_ref[...]
    project_kv()
    for g in range(N_KV_HEADS):
        cols = slice(g * GROUP_WIDTH, (g + 1) * GROUP_WIDTH)
        q_ref[:, cols] = (_dot(u, wq_refs[g][...]) * ATTN_SCALE).astype(_bf16)
        gate_ref[:, cols] = jax.nn.sigmoid(_dot(u, wga_refs[g][...]))
        attend(g)
    y_attn = _dot(a_ref[...], wao_ref[...])
    out_ref[...] = (mconv_ref[...].astype(_f32) + gate_ref[...] * y_attn).astype(_bf16)


def _attn_branch(u, mconv, w_in, wkv, wao, slopes, sinks, tiles_per_seq):
    n = u.shape[0]
    q_off = 3 * D_MODEL // GROUP_WIDTH
    ga_off = (D_MODEL * 5 + 2 * N_KV_HEADS * HEAD_DIM) // GROUP_WIDTH
    row = lambda i: (i, 0)
    resident = lambda shape, col=0: pl.BlockSpec(shape, lambda i: (0, col), pipeline_mode=pl.Buffered(1))
    w_in_groups = lambda off: [resident((D_MODEL, GROUP_WIDTH), off + g) for g in range(N_KV_HEADS)]
    smem = pl.BlockSpec(memory_space=pltpu.SMEM)
    kv_scratch = pltpu.VMEM((N_KV_HEADS, ROW_TILE + WINDOW, LANES), _bf16)
    return pl.pallas_call(
        functools.partial(_attn_kernel, tiles_per_seq=tiles_per_seq),
        grid=(n // ROW_TILE,),
        in_specs=[
            smem, smem,
            pl.BlockSpec((ROW_TILE, D_MODEL), row),
            pl.BlockSpec((ROW_TILE, D_MODEL), row),
            resident((D_MODEL, N_KV_HEADS * LANES)),
            resident((N_Q_HEADS * HEAD_DIM, D_MODEL)),
        ] + w_in_groups(q_off) + w_in_groups(ga_off),
        out_specs=pl.BlockSpec((ROW_TILE, D_MODEL), row),
        out_shape=jax.ShapeDtypeStruct((n, D_MODEL), _bf16),
        scratch_shapes=[
            pltpu.VMEM((ROW_TILE, D_MODEL), _f32),
            pltpu.VMEM((ROW_TILE, N_Q_HEADS * HEAD_DIM), _bf16),
            pltpu.VMEM((ROW_TILE, N_Q_HEADS * HEAD_DIM), _bf16),
            kv_scratch, kv_scratch, kv_scratch, kv_scratch,
        ],
        compiler_params=pltpu.CompilerParams(
            dimension_semantics=("arbitrary",),
            vmem_limit_bytes=VMEM_LIMIT_BYTES),
        name="attn_branch",
    )(slopes, sinks, u, mconv, wkv, wao, *([w_in] * (2 * N_KV_HEADS)))


def _oproj_kernel(h_ref, m_ref, w_ref, out_ref):
    out_ref[...] = h_ref[...] + _dot(m_ref[...], w_ref[...])


def _oproj(h, merged, w_out):
    n = h.shape[0]
    row = lambda i: (i, 0)
    return pl.pallas_call(
        _oproj_kernel,
        grid=(n // FFN_ROW_TILE,),
        in_specs=[
            pl.BlockSpec((FFN_ROW_TILE, D_MODEL), row),
            pl.BlockSpec((FFN_ROW_TILE, D_MODEL), row),
            pl.BlockSpec((D_MODEL, D_MODEL), lambda i: (0, 0), pipeline_mode=pl.Buffered(1)),
        ],
        out_specs=pl.BlockSpec((FFN_ROW_TILE, D_MODEL), row),
        out_shape=jax.ShapeDtypeStruct((n, D_MODEL), _f32),
        compiler_params=pltpu.CompilerParams(
            dimension_semantics=("arbitrary",),
            vmem_limit_bytes=VMEM_LIMIT_BYTES),
        name="oproj",
    )(h, merged, w_out)


def kernel(x, norm_ffn1, w_gate1, w_up1, w_down1, norm_mix, w_in, w_conv, w_conv_out, attn_sinks,
           w_attn_out, w_out, norm_ffn2, w_gate2, w_up2, w_down2, norm_final):
    batch, seq, d = x.shape
    depth = norm_ffn1.shape[0]
    assert d == D_MODEL and seq % ROW_TILE == 0 and ROW_TILE % WINDOW == 0
    tiles_per_seq = seq // ROW_TILE
    bf = lambda w: w.astype(_bf16)
    gain = lambda g: g.reshape(1, D_MODEL).astype(_f32)
    slopes = 2.0 ** (-8.0 * jnp.arange(1, N_Q_HEADS + 1, dtype=_f32) / N_Q_HEADS)
    k_off = 4 * D_MODEL
    v_off = k_off + N_KV_HEADS * HEAD_DIM

    h = x.reshape(batch * seq, D_MODEL)
    gfin = gain(norm_final)
    for l in range(depth):
        last = l == depth - 1
        w_in_l = bf(w_in[l])
        wk = w_in_l[:, k_off:v_off].reshape(D_MODEL, N_KV_HEADS, HEAD_DIM)
        wv = w_in_l[:, v_off:v_off + N_KV_HEADS * HEAD_DIM].reshape(D_MODEL, N_KV_HEADS, HEAD_DIM)
        wkv = jnp.concatenate([wk, wv], axis=-1).reshape(D_MODEL, N_KV_HEADS * LANES)

        h = _ffn(h, gain(norm_ffn1[l]), bf(w_gate1[l]), bf(w_up1[l]), bf(w_down1[l]), gfin, False)
        g_mix = gain(norm_mix[l])
        mconv, u_mix = _conv_branch(h, g_mix, w_in_l, w_conv[l].astype(_f32), bf(w_conv_out[l]), tiles_per_seq)
        merged = _attn_branch(u_mix, mconv, w_in_l, wkv, bf(w_attn_out[l]), slopes,
                              attn_sinks[l].astype(_f32), tiles_per_seq)
        h = _oproj(h, merged, bf(w_out[l]))
        h = _ffn(h, gain(norm_ffn2[l]), bf(w_gate2[l]), bf(w_up2[l]), bf(w_down2[l]), gfin, last)
    return h.reshape(batch, seq, D_MODEL)
```

```python
import functools

import jax
import jax.numpy as jnp
from jax import lax
from jax.experimental import pallas as pl
from jax.experimental.pallas import tpu as pltpu

D_MODEL = 2048
D_FF = 5632
CONV_K = 3
N_Q_HEADS = 32
N_KV_HEADS = 4
HEAD_DIM = 64
Q_GROUP = N_Q_HEADS // N_KV_HEADS
GROUP_WIDTH = Q_GROUP * HEAD_DIM
WINDOW = 128
RMS_EPS = 1e-5
ATTN_SCALE = HEAD_DIM ** -0.5

LANES = 128
SUBLANES = 8
VMEM_LIMIT_BYTES = 60 * 1024 * 1024

ROW_TILE = 512
FFN_ROW_TILE = 1024
FF_TILE = 512
NORM_ROWS = 128
CONV_TILE = 512
assert D_FF // FF_TILE >= 2 and D_MODEL // CONV_TILE >= 2

_f32 = jnp.float32
_bf16 = jnp.bfloat16


def _rms_norm_f32(x, gain):
    return x * lax.rsqrt(jnp.mean(x * x, axis=-1, keepdims=True) + RMS_EPS) * gain


def _dot(a, b):
    return jnp.dot(a, b, preferred_element_type=_f32)


def _ffn_kernel(h_ref, gain_ref, wg_ref, wu_ref, wd_ref, gfin_ref, out_ref, u_ref, *, final_norm):
    j = pl.program_id(1)
    last = pl.num_programs(1) - 1

    def swiglu_down():
        u = u_ref[...]
        halves = []
        for c0 in range(0, FF_TILE, FF_TILE // 2):
            g = _dot(u, wg_ref[:, c0:c0 + FF_TILE // 2])
            up = _dot(u, wu_ref[:, c0:c0 + FF_TILE // 2])
            halves.append((g * jax.nn.sigmoid(g) * up).astype(_bf16))
        return _dot(jnp.concatenate(halves, axis=1), wd_ref[...])

    @pl.when(j == 0)
    def _():
        u_ref[...] = _rms_norm_f32(h_ref[...], gain_ref[...]).astype(_bf16)
        out_ref[...] = swiglu_down()

    @pl.when(jnp.logical_and(j > 0, j < last))
    def _():
        out_ref[...] += swiglu_down()

    @pl.when(j == last)
    def _():
        out_ref[...] = h_ref[...] + 0.5 * (out_ref[...] + swiglu_down())
        if final_norm:
            def chunk(c, carry):
                rows = pl.ds(pl.multiple_of(c * NORM_ROWS, NORM_ROWS), NORM_ROWS)
                out_ref[rows, :] = _rms_norm_f32(out_ref[rows, :], gfin_ref[...])
                return carry
            lax.fori_loop(0, FFN_ROW_TILE // NORM_ROWS, chunk, 0)


def _ffn(h, gain, wg, wu, wd, gfin, final_norm):
    n = h.shape[0]
    row = lambda i, j: (i, 0)
    const = lambda i, j: (0, 0)
    return pl.pallas_call(
        functools.partial(_ffn_kernel, final_norm=final_norm),
        grid=(n // FFN_ROW_TILE, D_FF // FF_TILE),
        in_specs=[
            pl.BlockSpec((FFN_ROW_TILE, D_MODEL), row),
            pl.BlockSpec((1, D_MODEL), const),
            pl.BlockSpec((D_MODEL, FF_TILE), lambda i, j: (0, j)),
            pl.BlockSpec((D_MODEL, FF_TILE), lambda i, j: (0, j)),
            pl.BlockSpec((FF_TILE, D_MODEL), lambda i, j: (j, 0)),
            pl.BlockSpec((1, D_MODEL), const),
        ],
        out_specs=pl.BlockSpec((FFN_ROW_TILE, D_MODEL), row),
        out_shape=jax.ShapeDtypeStruct((n, D_MODEL), _f32),
        scratch_shapes=[pltpu.VMEM((FFN_ROW_TILE, D_MODEL), _bf16)],
        compiler_params=pltpu.CompilerParams(
            dimension_semantics=("arbitrary", "arbitrary"),
            vmem_limit_bytes=VMEM_LIMIT_BYTES),
        name="ffn_final" if final_norm else "ffn",
    )(h, gain, wg, wu, wd, gfin)


def _conv_kernel(h_ref, gain_ref, wproj_ref, taps_ref, wco_ref, out_ref, u_ref,
                 acc_ref, gate_ref, ucbuf_ref, carry_ref, *, tiles_per_seq):
    i = pl.program_id(0)
    j = pl.program_id(1)

    @pl.when(i % tiles_per_seq == 0)
    def _():
        carry_ref[j] = jnp.zeros((SUBLANES, CONV_TILE), _f32)

    def conv_out_part():
        proj = _dot(u_ref[...], wproj_ref[...])
        uc = proj[:, 0:CONV_TILE] * proj[:, CONV_TILE:2 * CONV_TILE]
        ucbuf_ref[0:SUBLANES, :] = carry_ref[j]
        ucbuf_ref[SUBLANES:, :] = uc
        carry_ref[j] = uc[ROW_TILE - SUBLANES:, :]
        b = proj[:, 2 * CONV_TILE:3 * CONV_TILE]
        col = pl.multiple_of(j * CONV_TILE, CONV_TILE)
        gate_ref[:, pl.ds(col, CONV_TILE)] = jax.nn.sigmoid(proj[:, 3 * CONV_TILE:])
        taps = taps_ref[...]
        conv = taps[CONV_K - 1:CONV_K, :] * uc
        for tap in range(CONV_K - 1):
            back = CONV_K - 1 - tap
            conv = conv + taps[tap:tap + 1, :] * ucbuf_ref[SUBLANES - back:SUBLANES - back + ROW_TILE, :]
        return _dot((b * conv).astype(_bf16), wco_ref[...])

    last = pl.num_programs(1) - 1

    @pl.when(j == 0)
    def _():
        u_ref[...] = _rms_norm_f32(h_ref[...], gain_ref[...]).astype(_bf16)
        acc_ref[...] = conv_out_part()

    @pl.when(jnp.logical_and(j > 0, j < last))
    def _():
        acc_ref[...] += conv_out_part()

    @pl.when(j == last)
    def _():
        acc = acc_ref[...] + conv_out_part()
        out_ref[...] = (gate_ref[...] * acc).astype(_bf16)


def _conv_branch(h, gain, wproj, taps, wco, tiles_per_seq):
    n = h.shape[0]
    nblk = D_MODEL // CONV_TILE
    row = lambda i, j: (i, 0)
    const = lambda i, j: (0, 0)
    return pl.pallas_call(
        functools.partial(_conv_kernel, tiles_per_seq=tiles_per_seq),
        grid=(n // ROW_TILE, nblk),
        in_specs=[
            pl.BlockSpec((ROW_TILE, D_MODEL), row),
            pl.BlockSpec((1, D_MODEL), const),
            pl.BlockSpec((D_MODEL, 4 * CONV_TILE), lambda i, j: (0, j)),
            pl.BlockSpec((CONV_K, CONV_TILE), lambda i, j: (0, j)),
            pl.BlockSpec((CONV_TILE, D_MODEL), lambda i, j: (j, 0)),
        ],
        out_specs=[pl.BlockSpec((ROW_TILE, D_MODEL), row), pl.BlockSpec((ROW_TILE, D_MODEL), row)],
        out_shape=[jax.ShapeDtypeStruct((n, D_MODEL), _bf16), jax.ShapeDtypeStruct((n, D_MODEL), _bf16)],
        scratch_shapes=[
            pltpu.VMEM((ROW_TILE, D_MODEL), _f32),
            pltpu.VMEM((ROW_TILE, D_MODEL), _f32),
            pltpu.VMEM((ROW_TILE + SUBLANES, CONV_TILE), _f32),
            pltpu.VMEM((nblk, SUBLANES, CONV_TILE), _f32),
        ],
        compiler_params=pltpu.CompilerParams(
            dimension_semantics=("arbitrary", "arbitrary"),
            vmem_limit_bytes=VMEM_LIMIT_BYTES),
        name="conv_branch",
    )(h, gain, wproj, taps, wco)


def _attn_kernel(slopes_ref, sinks_ref, u_ref, mconv_ref, wkv_ref, wao_ref, *refs, tiles_per_seq):
    wq_refs = refs[:N_KV_HEADS]
    wga_refs = refs[N_KV_HEADS:2 * N_KV_HEADS]
    out_ref, gate_ref, q_ref, a_ref, klo_ref, khi_ref, vlo_ref, vhi_ref = refs[2 * N_KV_HEADS:]
    i = pl.program_id(0)
    blocks = ROW_TILE // WINDOW
    pairs = Q_GROUP // 2
    first_tile = i % tiles_per_seq == 0
    kv_refs = (klo_ref, khi_ref, vlo_ref, vhi_ref)

    @pl.when(first_tile)
    def _():
        for ref in kv_refs:
            ref[:, 0:WINDOW, :] = jnp.zeros((N_KV_HEADS, WINDOW, LANES), _bf16)

    @pl.when(jnp.logical_not(first_tile))
    def _():
        for ref in kv_refs:
            ref[:, 0:WINDOW, :] = ref[:, ROW_TILE:ROW_TILE + WINDOW, :]

    def project_kv():
        kv_all = _dot(u_ref[...], wkv_ref[...])
        low = lax.broadcasted_iota(jnp.int32, (ROW_TILE, LANES), 1) < HEAD_DIM
        for g in range(N_KV_HEADS):
            kv = kv_all[:, g * LANES:(g + 1) * LANES]
            vk = pltpu.roll(kv, HEAD_DIM, axis=1)
            klo_ref[g, WINDOW:, :] = jnp.where(low, kv, 0.0).astype(_bf16)
            khi_ref[g, WINDOW:, :] = jnp.where(low, 0.0, vk).astype(_bf16)
            vlo_ref[g, WINDOW:, :] = jnp.where(low, vk, 0.0).astype(_bf16)
            vhi_ref[g, WINDOW:, :] = jnp.where(low, 0.0, kv).astype(_bf16)

    def attend(g):
        qi = lax.broadcasted_iota(jnp.int32, (WINDOW, WINDOW), 0)
        kc = lax.broadcasted_iota(jnp.int32, (WINDOW, WINDOW), 1)
        in_cur = kc <= qi
        dist = (qi - kc + jnp.where(in_cur, 0, WINDOW)).astype(_f32)
        low_lane = kc < HEAD_DIM
        neg_inf = jnp.float32(-jnp.inf)
        sum_r = lax.broadcasted_iota(jnp.int32, (4 * WINDOW, LANES), 0) < 2 * WINDOW
        sum_c = lax.broadcasted_iota(jnp.int32, (4 * WINDOW, LANES), 1) < HEAD_DIM
        sum_cols = jnp.where(sum_r == sum_c, 1.0, 0.0).astype(_bf16)

        for blk in range(blocks):
            r0 = blk * WINDOW
            cols = [slice(g * GROUP_WIDTH + pair * LANES, g * GROUP_WIDTH + (pair + 1) * LANES)
                    for pair in range(pairs)]
            q_rows = jnp.concatenate([q_ref[r0:r0 + WINDOW, c] for c in cols], axis=0)
            keys = jnp.concatenate(
                [klo_ref[g, r0:r0 + 2 * WINDOW, :], khi_ref[g, r0:r0 + 2 * WINDOW, :]], axis=0)
            s_all = lax.dot_general(q_rows, keys, (((1,), (1,)), ((), ())), preferred_element_type=_f32)
            prob_rows = []
            sink_terms = []
            for pair in range(pairs):
                prob_cols = []
                sink_pair = []
                for half in range(2):
                    head = g * Q_GROUP + 2 * pair + half
                    c0 = half * 2 * WINDOW
                    s_prev = s_all[pair * WINDOW:(pair + 1) * WINDOW, c0:c0 + WINDOW]
                    s_cur = s_all[pair * WINDOW:(pair + 1) * WINDOW, c0 + WINDOW:c0 + 2 * WINDOW]
                    if blk == 0:
                        s_prev = jnp.where(first_tile, neg_inf, s_prev)
                    s = jnp.where(in_cur, s_cur, s_prev) - slopes_ref[head] * dist
                    sink = sinks_ref[head]
                    m = jnp.maximum(jnp.max(s, axis=-1, keepdims=True), sink)
                    p = jnp.exp(s - m).astype(_bf16)
                    zero = jnp.zeros_like(p)
                    prob_cols += [jnp.where(in_cur, zero, p), jnp.where(in_cur, p, zero)]
                    sink_pair.append(jnp.exp(sink - m))
                prob_rows.append(jnp.concatenate(prob_cols, axis=1))
                sink_terms.append(jnp.where(low_lane, sink_pair[0], sink_pair[1]))
            probs = jnp.concatenate(prob_rows, axis=0)
            vals = jnp.concatenate(
                [vlo_ref[g, r0:r0 + 2 * WINDOW, :], vhi_ref[g, r0:r0 + 2 * WINDOW, :]], axis=0)
            o = _dot(probs, jnp.concatenate([vals, sum_cols], axis=1))
            for pair in range(pairs):
                rows = slice(pair * WINDOW, (pair + 1) * WINDOW)
                denom = o[rows, LANES:] + sink_terms[pair]
                a_ref[r0:r0 + WINDOW, cols[pair]] = (o[rows, :LANES] / denom).astype(_bf16)

    u = u_ref[...]
    project_kv()
    for g in range(N_KV_HEADS):
        cols = slice(g * GROUP_WIDTH, (g + 1) * GROUP_WIDTH)
        q_ref[:, cols] = (_dot(u, wq_refs[g][...]) * ATTN_SCALE).astype(_bf16)
        gate_ref[:, cols] = jax.nn.sigmoid(_dot(u, wga_refs[g][...]))
        attend(g)
    y_attn = _dot(a_ref[...], wao_ref[...])
    out_ref[...] = (mconv_ref[...].astype(_f32) + gate_ref[...] * y_attn).astype(_bf16)


def _attn_branch(u, mconv, w_q_ga, wkv, wao, slopes, sinks, tiles_per_seq):
    n = u.shape[0]
    q_off, ga_off = 0, N_KV_HEADS
    row = lambda i: (i, 0)
    resident = lambda shape, col=0: pl.BlockSpec(shape, lambda i: (0, col), pipeline_mode=pl.Buffered(1))
    w_in_groups = lambda off: [resident((D_MODEL, GROUP_WIDTH), off + g) for g in range(N_KV_HEADS)]
    smem = pl.BlockSpec(memory_space=pltpu.SMEM)
    kv_scratch = pltpu.VMEM((N_KV_HEADS, ROW_TILE + WINDOW, LANES), _bf16)
    return pl.pallas_call(
        functools.partial(_attn_kernel, tiles_per_seq=tiles_per_seq),
        grid=(n // ROW_TILE,),
        in_specs=[
            smem, smem,
            pl.BlockSpec((ROW_TILE, D_MODEL), row),
            pl.BlockSpec((ROW_TILE, D_MODEL), row),
            resident((D_MODEL, N_KV_HEADS * LANES)),
            resident((N_Q_HEADS * HEAD_DIM, D_MODEL)),
        ] + w_in_groups(q_off) + w_in_groups(ga_off),
        out_specs=pl.BlockSpec((ROW_TILE, D_MODEL), row),
        out_shape=jax.ShapeDtypeStruct((n, D_MODEL), _bf16),
        scratch_shapes=[
            pltpu.VMEM((ROW_TILE, D_MODEL), _f32),
            pltpu.VMEM((ROW_TILE, N_Q_HEADS * HEAD_DIM), _bf16),
            pltpu.VMEM((ROW_TILE, N_Q_HEADS * HEAD_DIM), _bf16),
            kv_scratch, kv_scratch, kv_scratch, kv_scratch,
        ],
        compiler_params=pltpu.CompilerParams(
            dimension_semantics=("arbitrary",),
            vmem_limit_bytes=VMEM_LIMIT_BYTES),
        name="attn_branch",
    )(slopes, sinks, u, mconv, wkv, wao, *([w_q_ga] * (2 * N_KV_HEADS)))


def _oproj_kernel(h_ref, m_ref, w_ref, out_ref):
    out_ref[...] = h_ref[...] + _dot(m_ref[...], w_ref[...])


def _oproj(h, merged, w_out):
    n = h.shape[0]
    row = lambda i: (i, 0)
    return pl.pallas_call(
        _oproj_kernel,
        grid=(n // FFN_ROW_TILE,),
        in_specs=[
            pl.BlockSpec((FFN_ROW_TILE, D_MODEL), row),
            pl.BlockSpec((FFN_ROW_TILE, D_MODEL), row),
            pl.BlockSpec((D_MODEL, D_MODEL), lambda i: (0, 0), pipeline_mode=pl.Buffered(1)),
        ],
        out_specs=pl.BlockSpec((FFN_ROW_TILE, D_MODEL), row),
        out_shape=jax.ShapeDtypeStruct((n, D_MODEL), _f32),
        compiler_params=pltpu.CompilerParams(
            dimension_semantics=("arbitrary",),
            vmem_limit_bytes=VMEM_LIMIT_BYTES),
        name="oproj",
    )(h, merged, w_out)


def kernel(x, norm_ffn1, w_gate1, w_up1, w_down1, norm_mix, w_in, w_conv, w_conv_out, attn_sinks,
           w_attn_out, w_out, norm_ffn2, w_gate2, w_up2, w_down2, norm_final):
    batch, seq, d = x.shape
    depth = norm_ffn1.shape[0]
    assert d == D_MODEL and seq % ROW_TILE == 0 and ROW_TILE % WINDOW == 0
    tiles_per_seq = seq // ROW_TILE
    bf = lambda w: w.astype(_bf16)
    gain = lambda g: g.reshape(1, D_MODEL).astype(_f32)
    slopes = 2.0 ** (-8.0 * jnp.arange(1, N_Q_HEADS + 1, dtype=_f32) / N_Q_HEADS)
    k_off = 4 * D_MODEL
    v_off = k_off + N_KV_HEADS * HEAD_DIM

    h = x.reshape(batch * seq, D_MODEL)
    gfin = gain(norm_final)
    for l in range(depth):
        last = l == depth - 1
        w_b, w_c, w_x = (w_in[l][:, t * D_MODEL:(t + 1) * D_MODEL] for t in range(3))
        w_q = w_in[l][:, 3 * D_MODEL:k_off]
        gc_off = v_off + N_KV_HEADS * HEAD_DIM
        w_gc, w_ga = w_in[l][:, gc_off:gc_off + D_MODEL], w_in[l][:, gc_off + D_MODEL:]
        chunks = [w.reshape(D_MODEL, D_MODEL // CONV_TILE, CONV_TILE) for w in (w_c, w_x, w_b, w_gc)]
        w_conv_proj = bf(jnp.stack(chunks, axis=2).reshape(D_MODEL, 4 * D_MODEL))
        w_q_ga = bf(jnp.concatenate([w_q, w_ga], axis=1))
        wk = w_in[l][:, k_off:v_off].reshape(D_MODEL, N_KV_HEADS, HEAD_DIM)
        wv = w_in[l][:, v_off:gc_off].reshape(D_MODEL, N_KV_HEADS, HEAD_DIM)
        wkv = bf(jnp.concatenate([wk, wv], axis=-1).reshape(D_MODEL, N_KV_HEADS * LANES))

        h = _ffn(h, gain(norm_ffn1[l]), bf(w_gate1[l]), bf(w_up1[l]), bf(w_down1[l]), gfin, False)
        g_mix = gain(norm_mix[l])
        mconv, u_mix = _conv_branch(h, g_mix, w_conv_proj, w_conv[l].astype(_f32), bf(w_conv_out[l]),
                                    tiles_per_seq)
        merged = _attn_branch(u_mix, mconv, w_q_ga, wkv, bf(w_attn_out[l]), slopes,
                              attn_sinks[l].astype(_f32), tiles_per_seq)
        h = _oproj(h, merged, bf(w_out[l]))
        h = _ffn(h, gain(norm_ffn2[l]), bf(w_gate2[l]), bf(w_up2[l]), bf(w_down2[l]), gfin, last)
    return h.reshape(batch, seq, D_MODEL)
```
